```python
import math, functools
import jax, jax.numpy as jnp
from jax import lax
import numpy as np

D_MODEL = 1024
BATCH = 1
SEQ = 16384
DEPTH = 1
DEC_BATCH = 128
DEC_SEQ = 1
PAST_LEN = 16384
PAGE_SIZE = 128

MLA_HEADS = 8
MLA_Q_LORA = 384
MLA_KV_LORA = 256
MLA_NOPE_DIM = 64
MLA_ROPE_DIM = 32
MLA_V_DIM = 64
DIFF_HEADS = 4
DIFF_HEAD_DIM = 64
DIFF_ROT_DIM = DIFF_HEAD_DIM // 4
ROPE_THETA = 500000.0
D_FF = 2816
N_MOD = 9
Q_BLOCK = 128
DEEPNORM_ALPHA = (2 * DEPTH) ** 0.25
DEEPNORM_BETA = (8 * DEPTH) ** -0.25
LN_EPS = 1e-5
RMS_EPS = 1e-6
MLA_SCALE = (MLA_NOPE_DIM + MLA_ROPE_DIM) ** -0.5
DIFF_SCALE = DIFF_HEAD_DIM ** -0.5
IN_WIDTHS = (MLA_Q_LORA, MLA_KV_LORA, MLA_ROPE_DIM,
             DIFF_HEADS * 2 * DIFF_HEAD_DIM, DIFF_HEADS * 2 * DIFF_HEAD_DIM, DIFF_HEADS * 2 * DIFF_HEAD_DIM,
             D_MODEL, D_MODEL)
IN_COLS = (MLA_Q_LORA + MLA_KV_LORA + MLA_ROPE_DIM + 3 * DIFF_HEADS * 2 * DIFF_HEAD_DIM + 2 * D_MODEL)

kernel_name = "hybrid_mla_diffattn_macaron_deepnorm_adaln_step"


def _layernorm(x, g, b):
    xf = x.astype(jnp.float32)
    mu = jnp.mean(xf, -1, keepdims=True)
    var = jnp.mean(jnp.square(xf - mu), -1, keepdims=True)
    return ((xf - mu) * lax.rsqrt(var + LN_EPS) * g + b).astype(x.dtype)


def _rmsnorm(x, g):
    xf = x.astype(jnp.float32)
    return (xf * lax.rsqrt(jnp.mean(jnp.square(xf), -1, keepdims=True) + RMS_EPS) * g).astype(x.dtype)


def _rope_tables(pos, dim):
    inv = ROPE_THETA ** (-jnp.arange(0, dim, 2, dtype=jnp.float32) / dim)
    ang = pos.astype(jnp.float32)[:, None] * inv[None, :]
    return jnp.cos(ang), jnp.sin(ang)


def _apply_rope(x, cos, sin):
    half = x.shape[-1] // 2
    xf = x.astype(jnp.float32)
    x1, x2 = xf[..., :half], xf[..., half:]
    return jnp.concatenate([x1 * cos - x2 * sin, x2 * cos + x1 * sin], -1).astype(x.dtype)


def _partial_rope(x, cos, sin):
    return jnp.concatenate([_apply_rope(x[..., :DIFF_ROT_DIM], cos, sin), x[..., DIFF_ROT_DIM:]], -1)


def _masked_softmax(s, mask):
    return jax.nn.softmax(jnp.where(mask, s.astype(jnp.float32), -jnp.inf), axis=-1)


def _split_softmax(s_past, s_new, mask):
    n_past = s_past.shape[-1]
    p = jax.nn.softmax(jnp.concatenate(
        [s_past.astype(jnp.float32), jnp.where(mask, s_new.astype(jnp.float32), -jnp.inf)], -1), axis=-1)
    return p[..., :n_past], p[..., n_past:]


def _swiglu(x, w1, w2):
    gate, up = jnp.split(x @ w1, 2, axis=-1)
    return (jax.nn.silu(gate) * up) @ w2


def _prompt_attention(q_nope, q_rope, ckv, krope, q_d, k_d, v_d, lam, *, w_uk, w_uv):
    n, s = ckv.shape[:2]
    k_nope = jnp.einsum('nsc,chd->nshd', ckv, w_uk)
    v_mla = jnp.einsum('nsc,chd->nshd', ckv, w_uv)
    k_pos = jnp.arange(s)

    def block(i):
        q0 = i * Q_BLOCK
        qn = lax.dynamic_slice_in_dim(q_nope, q0, Q_BLOCK, 1)
        qr = lax.dynamic_slice_in_dim(q_rope, q0, Q_BLOCK, 1)
        qd = lax.dynamic_slice_in_dim(q_d, q0, Q_BLOCK, 1)
        mask = k_pos[None, :] <= (q0 + jnp.arange(Q_BLOCK))[:, None]
        s_mla = (jnp.einsum('nqhd,nkhd->nhqk', qn, k_nope)
                 + jnp.einsum('nqhr,nkr->nhqk', qr, krope)) * MLA_SCALE
        p = _masked_softmax(s_mla, mask).astype(v_mla.dtype)
        o_mla = jnp.einsum('nhqk,nkhd->nqhd', p, v_mla)
        s_d = jnp.einsum('nqhjd,nkhjd->nhjqk', qd, k_d) * DIFF_SCALE
        p_d = _masked_softmax(s_d, mask)
        a = (p_d[:, :, 0] - lam * p_d[:, :, 1]).astype(v_d.dtype)
        o_d = jnp.einsum('nhqk,nkhe->nqhe', a, v_d)
        return o_mla, o_d

    o_mla, o_d = lax.map(block, jnp.arange(s // Q_BLOCK))

    def fold(o):
        return jnp.moveaxis(o, 0, 1).reshape((n, s) + o.shape[3:])

    return fold(o_mla), fold(o_d)


def _sample_attention(q_nope, q_rope, ckv, krope, q_d, k_d, v_d, lam, *, w_uk, w_uv, layer,
                      page_table, cache_ckv, cache_krope, cache_diff_k, cache_diff_v):
    t = q_nope.shape[1]
    q_abs = jnp.einsum('nthd,chd->nthc', q_nope, w_uk)
    new_mask = jnp.arange(t)[None, :] <= jnp.arange(t)[:, None]

    def one(args):
        pt, qa, qr, cn, kn, qd, kd, vd = args
        ckv_p = cache_ckv[layer, pt].reshape(-1, MLA_KV_LORA)
        kr_p = cache_krope[layer, pt].reshape(-1, MLA_ROPE_DIM)
        dk_p = cache_diff_k[layer, pt].reshape(-1, DIFF_HEADS, 2, DIFF_HEAD_DIM)
        dv_p = cache_diff_v[layer, pt].reshape(-1, DIFF_HEADS, 2 * DIFF_HEAD_DIM)
        s_past = (jnp.einsum('thc,kc->htk', qa, ckv_p) + jnp.einsum('thr,kr->htk', qr, kr_p)) * MLA_SCALE
        s_new = (jnp.einsum('thc,kc->htk', qa, cn) + jnp.einsum('thr,kr->htk', qr, kn)) * MLA_SCALE
        p_past, p_new = _split_softmax(s_past, s_new, new_mask)
        o_lat = (jnp.einsum('htk,kc->thc', p_past.astype(ckv_p.dtype), ckv_p)
                 + jnp.einsum('htk,kc->thc', p_new.astype(cn.dtype), cn))
        sd_past = jnp.einsum('thjd,khjd->hjtk', qd, dk_p) * DIFF_SCALE
        sd_new = jnp.einsum('thjd,khjd->hjtk', qd, kd) * DIFF_SCALE
        pd_past, pd_new = _split_softmax(sd_past, sd_new, new_mask)
        a_past = (pd_past[:, 0] - lam * pd_past[:, 1]).astype(dv_p.dtype)
        a_new = (pd_new[:, 0] - lam * pd_new[:, 1]).astype(vd.dtype)
        o_d = jnp.einsum('htk,khe->the', a_past, dv_p) + jnp.einsum('htk,khe->the', a_new, vd)
        return o_lat, o_d

    o_lat, o_d = lax.map(one, (page_table, q_abs, q_rope, ckv, krope, q_d, k_d, v_d))
    o_mla = jnp.einsum('nthc,chd->nthd', o_lat, w_uv)
    return o_mla, o_d


def _layer(x, c, pos, attend, lambda_init, w_ada, b_ada, ln_g, ln_b, ffn_w1, ffn_w2, w_in,
           q_norm_g, kv_norm_g, w_uq, diff_lambda, subln_g, w_br_mla, w_br_diff, w_out):
    n, t, d = x.shape
    mod = (jax.nn.silu(c) @ w_ada + b_ada).reshape(n, N_MOD, 1, d)
    sh1, sc1, g1, sh2, sc2, g2, sh3, sc3, g3 = (mod[:, i] for i in range(N_MOD))

    f1 = _swiglu(x * (1.0 + sc1) + sh1, ffn_w1[0], ffn_w2[0])
    x = _layernorm(DEEPNORM_ALPHA * x + 0.5 * (1.0 + g1) * f1, ln_g[0], ln_b[0])

    h = x * (1.0 + sc2) + sh2
    offs = np.cumsum(IN_WIDTHS)[:-1].tolist()
    q_lat, ckv, krope, q_d, k_d, v_d, gl_mla, gl_diff = jnp.split(h @ w_in, offs, axis=-1)
    cos_m, sin_m = _rope_tables(pos, MLA_ROPE_DIM)
    cos_d, sin_d = _rope_tables(pos, DIFF_ROT_DIM)
    q = jnp.einsum('ntq,qhe->nthe', _rmsnorm(q_lat, q_norm_g), w_uq)
    q_nope = q[..., :MLA_NOPE_DIM]
    q_rope = _apply_rope(q[..., MLA_NOPE_DIM:], cos_m[:, None], sin_m[:, None])
    ckv = _rmsnorm(ckv, kv_norm_g)
    krope = _apply_rope(krope, cos_m, sin_m)
    q_d = _partial_rope(q_d.reshape(n, t, DIFF_HEADS, 2, DIFF_HEAD_DIM), cos_d[:, None, None], sin_d[:, None, None])
    k_d = _partial_rope(k_d.reshape(n, t, DIFF_HEADS, 2, DIFF_HEAD_DIM), cos_d[:, None, None], sin_d[:, None, None])
    v_d = v_d.reshape(n, t, DIFF_HEADS, 2 * DIFF_HEAD_DIM)
    dl = diff_lambda.astype(jnp.float32)
    lam = jnp.exp(jnp.sum(dl[0] * dl[1])) - jnp.exp(jnp.sum(dl[2] * dl[3])) + lambda_init

    o_mla, o_d = attend(q_nope, q_rope, ckv, krope, q_d, k_d, v_d, lam)
    o_d = _rmsnorm(o_d, subln_g) * (1.0 - lambda_init)
    b_mla = o_mla.reshape(n, t, -1) @ w_br_mla
    b_diff = o_d.reshape(n, t, -1) @ w_br_diff
    mix = (jax.nn.sigmoid(gl_mla) * b_mla + jax.nn.sigmoid(gl_diff) * b_diff) @ w_out
    x = _layernorm(DEEPNORM_ALPHA * x + (1.0 + g2) * mix, ln_g[1], ln_b[1])

    f2 = _swiglu(x * (1.0 + sc3) + sh3, ffn_w1[1], ffn_w2[1])
    x = _layernorm(DEEPNORM_ALPHA * x + 0.5 * (1.0 + g3) * f2, ln_g[2], ln_b[2])
    return x, ckv, krope, k_d, v_d


def _normal(k, shape, scale):
    return jax.random.normal(k, shape, jnp.float32) * scale


def setup_inputs(seed: int = 0) -> dict:
    key = jax.random.key(seed)
    ks = jax.random.split(key, 32)
    n_pages = PAST_LEN // PAGE_SIZE
    n_used = DEC_BATCH * n_pages
    n_pool = n_used + max(1, n_used // 4)
    page_table = jax.random.permutation(ks[0], n_pool)[:n_used].reshape(DEC_BATCH, n_pages).astype(jnp.int32)
    d, f = D_MODEL, D_FF
    hv_mla = MLA_HEADS * MLA_V_DIM
    hv_diff = DIFF_HEADS * 2 * DIFF_HEAD_DIM
    return {
        "x_prompt": _normal(ks[1], (BATCH, SEQ, d), 1.0),
        "x_sample": _normal(ks[2], (DEC_BATCH, DEC_SEQ, d), 1.0),
        "cache_ckv": _normal(ks[3], (DEPTH, n_pool, PAGE_SIZE, MLA_KV_LORA), 1.0),
        "cache_krope": _normal(ks[4], (DEPTH, n_pool, PAGE_SIZE, MLA_ROPE_DIM), 1.0),
        "cache_diff_k": _normal(ks[5], (DEPTH, n_pool, PAGE_SIZE, DIFF_HEADS, 2, DIFF_HEAD_DIM), 1.0),
        "cache_diff_v": _normal(ks[6], (DEPTH, n_pool, PAGE_SIZE, DIFF_HEADS, 2 * DIFF_HEAD_DIM), 1.0),
        "page_table": page_table,
        "c_prompt": _normal(ks[7], (BATCH, d), 1.0),
        "c_sample": _normal(ks[8], (DEC_BATCH, d), 1.0),
        "w_ada": _normal(ks[9], (DEPTH, d, N_MOD * d), 0.2 * d ** -0.5),
        "b_ada": _normal(ks[10], (DEPTH, N_MOD * d), 0.02),
        "ln_g": 1.0 + _normal(ks[11], (DEPTH, 3, d), 0.02),
        "ln_b": _normal(ks[12], (DEPTH, 3, d), 0.02),
        "ffn_w1": _normal(ks[13], (DEPTH, 2, d, 2 * f), d ** -0.5),
        "ffn_w2": _normal(ks[14], (DEPTH, 2, f, d), DEEPNORM_BETA * f ** -0.5),
        "w_in": _normal(ks[15], (DEPTH, d, IN_COLS), d ** -0.5),
        "q_norm_g": 1.0 + _normal(ks[16], (DEPTH, MLA_Q_LORA), 0.02),
        "kv_norm_g": 1.0 + _normal(ks[17], (DEPTH, MLA_KV_LORA), 0.02),
        "w_uq": _normal(ks[18], (DEPTH, MLA_Q_LORA, MLA_HEADS, MLA_NOPE_DIM + MLA_ROPE_DIM), MLA_Q_LORA ** -0.5),
        "w_uk": _normal(ks[19], (DEPTH, MLA_KV_LORA, MLA_HEADS, MLA_NOPE_DIM), MLA_KV_LORA ** -0.5),
        "w_uv": _normal(ks[20], (DEPTH, MLA_KV_LORA, MLA_HEADS, MLA_V_DIM), MLA_KV_LORA ** -0.5),
        "diff_lambda": _normal(ks[21], (DEPTH, 4, DIFF_HEAD_DIM), 0.1),
        "subln_g": 1.0 + _normal(ks[22], (DEPTH, 2 * DIFF_HEAD_DIM), 0.02),
        "w_br_mla": _normal(ks[23], (DEPTH, hv_mla, d), hv_mla ** -0.5),
        "w_br_diff": _normal(ks[24], (DEPTH, hv_diff, d), hv_diff ** -0.5),
        "w_out": _normal(ks[25], (DEPTH, d, d), DEEPNORM_BETA * d ** -0.5),
    }


def reference(x_prompt, x_sample, cache_ckv, cache_krope, cache_diff_k, cache_diff_v, page_table,
              c_prompt, c_sample, w_ada, b_ada, ln_g, ln_b, ffn_w1, ffn_w2, w_in, q_norm_g, kv_norm_g,
              w_uq, w_uk, w_uv, diff_lambda, subln_g, w_br_mla, w_br_diff, w_out):
    past_len = page_table.shape[1] * cache_ckv.shape[2]
    pos_prompt = jnp.arange(x_prompt.shape[1], dtype=jnp.int32)
    pos_sample = past_len + jnp.arange(x_sample.shape[1], dtype=jnp.int32)
    y_p, y_s = x_prompt, x_sample
    rows_p = [[], [], [], []]
    rows_s = [[], [], [], []]
    for l in range(DEPTH):
        lambda_init = 0.8 - 0.6 * math.exp(-0.3 * l)
        shared = dict(w_ada=w_ada[l], b_ada=b_ada[l], ln_g=ln_g[l], ln_b=ln_b[l], ffn_w1=ffn_w1[l],
                      ffn_w2=ffn_w2[l], w_in=w_in[l], q_norm_g=q_norm_g[l], kv_norm_g=kv_norm_g[l],
                      w_uq=w_uq[l], diff_lambda=diff_lambda[l], subln_g=subln_g[l],
                      w_br_mla=w_br_mla[l], w_br_diff=w_br_diff[l], w_out=w_out[l])
        attend_p = functools.partial(_prompt_attention, w_uk=w_uk[l], w_uv=w_uv[l])
        attend_s = functools.partial(_sample_attention, w_uk=w_uk[l], w_uv=w_uv[l], layer=l,
                                     page_table=page_table, cache_ckv=cache_ckv, cache_krope=cache_krope,
                                     cache_diff_k=cache_diff_k, cache_diff_v=cache_diff_v)
        y_p, ckv_p, kr_p, dk_p, dv_p = _layer(y_p, c_prompt, pos_prompt, attend_p, lambda_init, **shared)
        y_s, ckv_s, kr_s, dk_s, dv_s = _layer(y_s, c_sample, pos_sample, attend_s, lambda_init, **shared)
        for acc, r in zip(rows_p, (ckv_p, kr_p, dk_p, dv_p)):
            acc.append(r)
        for acc, r in zip(rows_s, (ckv_s, kr_s, dk_s, dv_s)):
            acc.append(r)
    new_ckv_prompt, new_krope_prompt, new_dk_prompt, new_dv_prompt = [jnp.stack(r, 0) for r in rows_p]
    new_ckv_sample, new_krope_sample, new_dk_sample, new_dv_sample = [jnp.stack(r, 0) for r in rows_s]
    return (y_p, y_s, new_ckv_prompt, new_krope_prompt, new_dk_prompt, new_dv_prompt,
            new_ckv_sample, new_krope_sample, new_dk_sample, new_dv_sample)
```

```python
import functools
import math

import jax
import jax.numpy as jnp
from jax import lax
from jax.experimental import pallas as pl
from jax.experimental.pallas import tpu as pltpu

F32 = jnp.float32
BF16 = jnp.bfloat16

D_MODEL = 1024
D_FF = 2816
N_MOD = 9
MLA_HEADS = 8
MLA_Q_LORA = 384
MLA_KV_LORA = 256
MLA_NOPE = 64
MLA_ROPE = 32
MLA_V = 64
DIFF_HEADS = 4
DIFF_HD = 64
DIFF_ROT = 16
ROPE_THETA = 500000.0
DEPTH = 1
ALPHA = (2 * DEPTH) ** 0.25
LN_EPS = 1e-5
RMS_EPS = 1e-6
MLA_SCALE = (MLA_NOPE + MLA_ROPE) ** -0.5
DIFF_SCALE = DIFF_HD ** -0.5
LAMBDA_INIT = 0.8 - 0.6 * math.exp(-0.3 * 0)
PAGE = 128
LANES = 128
NEG = -1e30

VMEM_LIMIT = 56 * 1024 * 1024
C_QLAT, C_CKV, C_KR4, C_QD, C_QDSW, C_KD, C_KDSW, C_VD, C_END = (
    0, 384, 640, 768, 1280, 1792, 2304, 2816, 3328)


def _params(sem=None):
    return pltpu.CompilerParams(dimension_semantics=sem, vmem_limit_bytes=VMEM_LIMIT)


def _const_spec(shape):
    nd = len(shape)
    return pl.BlockSpec(shape, lambda *_: (0,) * nd, pipeline_mode=pl.Buffered(1))


def _layernorm(y, g, b):
    mu = jnp.mean(y, axis=-1, keepdims=True)
    d = y - mu
    var = jnp.mean(d * d, axis=-1, keepdims=True)
    return d * lax.rsqrt(var + LN_EPS) * g + b


def _rms_scale(x):
    return lax.rsqrt(jnp.mean(x * x, axis=-1, keepdims=True) + RMS_EPS)


def _swiglu(xm, w1_ref, w2_ref):
    f = w2_ref.shape[0]
    fc = f // 2
    acc = None
    for j in range(2):
        gate = jnp.dot(xm, w1_ref[:, j * fc:(j + 1) * fc], preferred_element_type=F32)
        up = jnp.dot(xm, w1_ref[:, f + j * fc:f + (j + 1) * fc], preferred_element_type=F32)
        act = (gate * jax.nn.sigmoid(gate) * up).astype(BF16)
        part = jnp.dot(act, w2_ref[j * fc:(j + 1) * fc, :], preferred_element_type=F32)
        acc = part if acc is None else acc + part
    return acc


def _adaln_kernel(c_ref, w_ref, b_ref, o_ref):
    c = c_ref[...]
    a = (c * jax.nn.sigmoid(c)).astype(BF16)
    o_ref[...] = jnp.dot(a, w_ref[...].astype(BF16), preferred_element_type=F32) + b_ref[...]


def _adaln(c_all, w_ada, b_ada):
    r = c_all.shape[0]
    n = w_ada.shape[1]
    tn = D_MODEL
    return pl.pallas_call(
        _adaln_kernel,
        grid=(n // tn,),
        in_specs=[pl.BlockSpec((r, D_MODEL), lambda j: (0, 0)),
                  pl.BlockSpec((D_MODEL, tn), lambda j: (0, j)),
                  pl.BlockSpec((1, tn), lambda j: (0, j))],
        out_specs=pl.BlockSpec((r, tn), lambda j: (0, j)),
        out_shape=jax.ShapeDtypeStruct((r, n), F32),
        compiler_params=_params(("arbitrary",)),
        name="adaln",
    )(c_all, w_ada, b_ada)


def _mod_spec(mod, k, tm):
    if mod.shape[0] == 1:
        return pl.BlockSpec((1, D_MODEL), lambda i: (0, k))
    return pl.BlockSpec((tm, D_MODEL), lambda i: (i, k))


def _ffn_ln_kernel(x_ref, sc_ref, sh_ref, g_ref, lng_ref, lnb_ref, w1_ref, w2_ref, o_ref):
    x = x_ref[...]
    xm = (x * (1.0 + sc_ref[...]) + sh_ref[...]).astype(BF16)
    f = _swiglu(xm, w1_ref, w2_ref)
    y = ALPHA * x + 0.5 * (1.0 + g_ref[...]) * f
    o_ref[...] = _layernorm(y, lng_ref[...], lnb_ref[...])


def _ffn_ln(x, mod, ln_g, ln_b, w1, w2, tm):
    n = x.shape[0]
    return pl.pallas_call(
        _ffn_ln_kernel,
        grid=(n // tm,),
        in_specs=[pl.BlockSpec((tm, D_MODEL), lambda i: (i, 0)),
                  _mod_spec(mod, 1, tm), _mod_spec(mod, 0, tm), _mod_spec(mod, 2, tm),
                  _const_spec(ln_g.shape), _const_spec(ln_b.shape),
                  _const_spec(w1.shape), _const_spec(w2.shape)],
        out_specs=pl.BlockSpec((tm, D_MODEL), lambda i: (i, 0)),
        out_shape=jax.ShapeDtypeStruct((n, D_MODEL), F32),
        compiler_params=_params(("arbitrary",)),
        name="ffn_ln",
    )(x, mod, mod, mod, ln_g, ln_b, w1, w2)


def _rope_tables(pos, invm, invd, tm):
    lane = lax.broadcasted_iota(jnp.int32, (tm, LANES), 1)
    ang = pos * invm
    sgn = jnp.where((lane % 32) < 16, -1.0, 1.0)
    tk = jnp.where((lane // 32) % 2 == 0, jnp.cos(ang), jnp.sin(ang) * sgn)
    angd = pos * invd
    sgnd = jnp.where((lane % 64) < 8, -1.0, 1.0)
    return lane, tk, jnp.cos(angd), jnp.sin(angd) * sgnd


def _mixer_in_body(prompt, x_ref, sc_ref, sh_ref, pos_ref, invm_ref, invd_ref, wc_ref, qg_ref, kvg_ref,
                   wuq_ref, wk_ref, wuv_ref, outs):
    tm = x_ref.shape[0]
    x = x_ref[...]
    h = (x * (1.0 + sc_ref[...]) + sh_ref[...]).astype(BF16)
    proj = jnp.dot(h, wc_ref[...], preferred_element_type=F32)
    q_lat = proj[:, C_QLAT:C_CKV]
    ckv = proj[:, C_CKV:C_KR4]
    kr4 = proj[:, C_KR4:C_QD]
    q_d = proj[:, C_QD:C_QDSW]
    q_dsw = proj[:, C_QDSW:C_KD]
    k_d = proj[:, C_KD:C_KDSW]
    k_dsw = proj[:, C_KDSW:C_VD]
    v_d = proj[:, C_VD:C_END]

    lane, tk, cd, sd = _rope_tables(pos_ref[...], invm_ref[...], invd_ref[...], tm)
    cd4 = jnp.concatenate([cd] * 4, axis=1)
    sd4 = jnp.concatenate([sd] * 4, axis=1)
    q_d_rot = (q_d * cd4 + q_dsw * sd4) * DIFF_SCALE
    k_d_rot = k_d * cd4 + k_dsw * sd4

    kr_t = kr4 * tk
    kr_rot = kr_t + pltpu.roll(kr_t, 96, 1)

    ckv_n = ckv * _rms_scale(ckv) * kvg_ref[...]
    ckv_b = ckv_n.astype(BF16)
    qn = (q_lat * _rms_scale(q_lat) * qg_ref[...]).astype(BF16)
    tq = jnp.where(lane < MLA_NOPE, MLA_SCALE, tk * MLA_SCALE)
    q_full = jnp.dot(qn, wuq_ref[...], preferred_element_type=F32) * jnp.concatenate([tq] * MLA_HEADS, axis=1)

    if prompt:
        (ckv_o, krT_o, dkT_o, dv_o, qm_o, km_o, qd_o, kd_o, vTm_o, vTd_o) = outs
        ckv_o[...] = ckv_n
        krT_o[...] = kr_rot.T[:MLA_ROPE, :]
        dkT_o[...] = k_d_rot.T
        dv_o[...] = v_d
        qm_o[...] = q_full.astype(BF16)
        kr_hi = jnp.where(lane >= MLA_NOPE, kr_rot, 0.0)
        k_nope = jnp.dot(ckv_b, wk_ref[...], preferred_element_type=F32)
        km_o[...] = (k_nope + jnp.concatenate([kr_hi] * MLA_HEADS, axis=1)).astype(BF16)
        qd_o[...] = q_d_rot.astype(BF16)
        kd_o[...] = k_d_rot.astype(BF16)
        v_mla = jnp.dot(ckv_b, wuv_ref[...], preferred_element_type=F32)
        vTm_o[0] = v_mla.T.astype(BF16)
        vTd_o[0] = v_d.T.astype(BF16)
    else:
        (ckv_o, kr_o, dk_o, dv_o, qcat_o, qd_o) = outs
        ckv_o[...] = ckv_n
        kr_o[...] = kr_rot[:, :MLA_ROPE]
        dk_o[...] = k_d_rot
        dv_o[...] = v_d
        qd_o[...] = q_d_rot
        for hh in range(MLA_HEADS):
            blk = q_full[:, hh * LANES:(hh + 1) * LANES]
            q_abs = jnp.dot(blk.astype(BF16), wk_ref[hh], preferred_element_type=F32)
            r = blk + pltpu.roll(blk, 96, 1)
            q_rot = jnp.where(lane < MLA_ROPE, pltpu.roll(r, 64, 1), 0.0)
            qcat_o[:, hh * 384:hh * 384 + 256] = q_abs
            qcat_o[:, hh * 384 + 256:(hh + 1) * 384] = q_rot


def _mixer_in_prompt_kernel(x_ref, sc_ref, sh_ref, pos_ref, invm_ref, invd_ref, wc_ref, qg_ref, kvg_ref,
                            wuq_ref, wk_ref, wuv_ref, *outs):
    _mixer_in_body(True, x_ref, sc_ref, sh_ref, pos_ref, invm_ref, invd_ref, wc_ref, qg_ref, kvg_ref,
                   wuq_ref, wk_ref, wuv_ref, outs)


def _mixer_in_sample_kernel(x_ref, sc_ref, sh_ref, pos_ref, invm_ref, invd_ref, wc_ref, qg_ref, kvg_ref,
                            wuq_ref, wk_ref, *outs):
    _mixer_in_body(False, x_ref, sc_ref, sh_ref, pos_ref, invm_ref, invd_ref, wc_ref, qg_ref, kvg_ref,
                   wuq_ref, wk_ref, None, outs)


def _mixer_in(prompt, x, mod, pos, invm, invd, wc, qg, kvg, wuq, wk, wuv, tm):
    n = x.shape[0]
    nt = n // tm
    row = lambda w: pl.BlockSpec((tm, w), lambda i: (i, 0))
    colT = lambda r: pl.BlockSpec((r, tm), lambda i: (0, i))
    in_specs = [row(D_MODEL), _mod_spec(mod, 4, tm), _mod_spec(mod, 3, tm),
                pl.BlockSpec((tm, 1), lambda i: (i, 0)),
                _const_spec(invm.shape), _const_spec(invd.shape), _const_spec(wc.shape),
                _const_spec(qg.shape), _const_spec(kvg.shape), _const_spec(wuq.shape), _const_spec(wk.shape)]
    args = [x, mod, mod, pos, invm, invd, wc, qg, kvg, wuq, wk]
    sds = jax.ShapeDtypeStruct
    if prompt:
        in_specs.append(_const_spec(wuv.shape))
        args.append(wuv)
        out_specs = [row(256), colT(MLA_ROPE), colT(512), row(512), row(1024), row(1024), row(512), row(512),
                     pl.BlockSpec((1, 512, tm), lambda i: (i, 0, 0)),
                     pl.BlockSpec((1, 512, tm), lambda i: (i, 0, 0))]
        out_shape = [sds((n, 256), F32), sds((MLA_ROPE, n), F32), sds((512, n), F32), sds((n, 512), F32),
                     sds((n, 1024), BF16), sds((n, 1024), BF16), sds((n, 512), BF16), sds((n, 512), BF16),
                     sds((nt, 512, tm), BF16), sds((nt, 512, tm), BF16)]
        body = _mixer_in_prompt_kernel
    else:
        out_specs = [row(256), row(MLA_ROPE), row(512), row(512), row(MLA_HEADS * 384), row(512)]
        out_shape = [sds((n, 256), F32), sds((n, MLA_ROPE), F32), sds((n, 512), F32), sds((n, 512), F32),
                     sds((n, MLA_HEADS * 384), F32), sds((n, 512), F32)]
        body = _mixer_in_sample_kernel
    return pl.pallas_call(
        body, grid=(nt,), in_specs=in_specs, out_specs=out_specs, out_shape=out_shape,
        compiler_params=_params(("arbitrary",)),
        name="mixer_in_prompt" if prompt else "mixer_in_sample",
    )(*args)


def _attn_kernel(q_ref, k_ref, vT_ref, o_ref, *, blk, dv, half_axis):
    qi = pl.program_id(half_axis + 1 if half_axis is not None else 1)
    q = q_ref[...]
    if half_axis is not None:
        lane = lax.broadcasted_iota(jnp.int32, q.shape, 1)
        q = jnp.where((lane // DIFF_HD) == pl.program_id(half_axis), q, jnp.zeros_like(q))

    def step(kb, carry, diagonal):
        m, l, acc = carry
        start = pl.multiple_of(kb * blk, blk)
        kblk = k_ref[pl.ds(start, blk), :]
        s = lax.dot_general(kblk, q, (((1,), (1,)), ((), ())), preferred_element_type=F32)
        if diagonal:
            key = lax.broadcasted_iota(jnp.int32, s.shape, 0)
            qry = lax.broadcasted_iota(jnp.int32, s.shape, 1)
            s = jnp.where(key <= qry, s, NEG)
        m_new = jnp.maximum(m, jnp.max(s, axis=0, keepdims=True))
        alpha = jnp.exp(m - m_new)
        p = jnp.exp(s - m_new)
        l = alpha * l + jnp.sum(p, axis=0, keepdims=True)
        acc = alpha * acc + jnp.dot(vT_ref[kb], p.astype(BF16), preferred_element_type=F32)
        return m_new, l, acc

    init = (jnp.full((1, blk), NEG, F32), jnp.zeros((1, blk), F32), jnp.zeros((dv, blk), F32))
    carry = lax.fori_loop(0, qi, lambda kb, c: step(kb, c, False), init)
    _, l, acc = step(qi, carry, True)
    o_ref[...] = (acc / l).astype(o_ref.dtype)


def _attn_mla(q, k, vT, blk):
    s = q.shape[0]
    nb = s // blk
    return pl.pallas_call(
        functools.partial(_attn_kernel, blk=blk, dv=MLA_V, half_axis=None),
        grid=(MLA_HEADS, nb),
        in_specs=[pl.BlockSpec((blk, LANES), lambda h, i: (i, h)),
                  pl.BlockSpec((s, LANES), lambda h, i: (0, h)),
                  pl.BlockSpec((nb, MLA_V, blk), lambda h, i: (0, h, 0))],
        out_specs=pl.BlockSpec((MLA_V, blk), lambda h, i: (h, i)),
        out_shape=jax.ShapeDtypeStruct((MLA_HEADS * MLA_V, s), BF16),
        compiler_params=_params(("arbitrary", "arbitrary")),
        name="attn_mla",
    )(q, k, vT)


def _attn_diff(q, k, vT, blk):
    s = q.shape[0]
    nb = s // blk
    dv = 2 * DIFF_HD
    return pl.pallas_call(
        functools.partial(_attn_kernel, blk=blk, dv=dv, half_axis=1),
        grid=(DIFF_HEADS, 2, nb),
        in_specs=[pl.BlockSpec((blk, LANES), lambda h, j, i: (i, h)),
                  pl.BlockSpec((s, LANES), lambda h, j, i: (0, h)),
                  pl.BlockSpec((nb, dv, blk), lambda h, j, i: (0, h, 0))],
        out_specs=pl.BlockSpec((dv, blk), lambda h, j, i: (2 * h + j, i)),
        out_shape=jax.ShapeDtypeStruct((DIFF_HEADS * 2 * dv, s), F32),
        compiler_params=_params(("arbitrary", "arbitrary", "arbitrary")),
        name="attn_diff",
    )(q, k, vT)


def _decode_kernel(pt_ref, q_ref, qd_ref, ckvn_ref, krn_ref, dkn_ref, dvn_ref,
                   ckv_hbm, krT_hbm, dkT_hbm, dv_hbm, olat_ref, od_ref,
                   bckv, bkr, bdk, bdv, sem, m1, l1, a1, m2, l2, a2, *, pages, nchunk, nreq):
    b = pl.program_id(0)
    c = pl.program_id(1)
    t = b * nchunk + c
    slot = t % 2
    kc = pages * PAGE

    def copies(bb, cc, sl):
        out = []
        for g in range(pages):
            pid = pt_ref[bb, cc * pages + g]
            out.append(pltpu.make_async_copy(ckv_hbm.at[pid], bckv.at[sl, pl.ds(g * PAGE, PAGE), :], sem.at[sl, 0]))
            out.append(pltpu.make_async_copy(krT_hbm.at[pid], bkr.at[sl, :, pl.ds(g * PAGE, PAGE)], sem.at[sl, 1]))
            out.append(pltpu.make_async_copy(dkT_hbm.at[pid], bdk.at[sl, :, pl.ds(g * PAGE, PAGE)], sem.at[sl, 2]))
            out.append(pltpu.make_async_copy(dv_hbm.at[pid], bdv.at[sl, pl.ds(g * 4 * PAGE, 4 * PAGE), :],
                                             sem.at[sl, 3]))
        return out

    @pl.when(t == 0)
    def _():
        for cp in copies(0, 0, 0):
            cp.start()

    @pl.when(t + 1 < nreq * nchunk)
    def _():
        wrap = c + 1 == nchunk
        nb = jnp.where(wrap, b + 1, b)
        nc = jnp.where(wrap, 0, c + 1)
        for cp in copies(nb, nc, 1 - slot):
            cp.start()

    @pl.when(c == 0)
    def _():
        m1[...] = jnp.full(m1.shape, NEG, F32)
        l1[...] = jnp.zeros(l1.shape, F32)
        a1[...] = jnp.zeros(a1.shape, F32)
        m2[...] = jnp.full(m2.shape, NEG, F32)
        l2[...] = jnp.zeros(l2.shape, F32)
        a2[...] = jnp.zeros(a2.shape, F32)

    for cp in copies(b, c, slot):
        cp.wait()

    q = q_ref[...]
    qa = q[:, :MLA_KV_LORA]
    qr = q[:, MLA_KV_LORA:MLA_KV_LORA + MLA_ROPE]
    row = lax.broadcasted_iota(jnp.int32, (8, 512), 0)
    blkid = lax.broadcasted_iota(jnp.int32, (8, 512), 1) // DIFF_HD
    qbd = jnp.where(row == blkid, jnp.broadcast_to(qd_ref[...], (8, 512)), 0.0)

    ckv = bckv[slot]
    s = lax.dot_general(qa, ckv, (((1,), (1,)), ((), ())), preferred_element_type=F32)
    s = s + jnp.dot(qr, bkr[slot], preferred_element_type=F32)
    m_new = jnp.maximum(m1[...], jnp.max(s, axis=1, keepdims=True))
    alpha = jnp.exp(m1[...] - m_new)
    p = jnp.exp(s - m_new)
    l1[...] = alpha * l1[...] + jnp.sum(p, axis=1, keepdims=True)
    a1[...] = alpha * a1[...] + jnp.dot(p, ckv, preferred_element_type=F32)
    m1[...] = m_new

    s = jnp.dot(qbd, bdk[slot], preferred_element_type=F32)
    m_new = jnp.maximum(m2[...], jnp.max(s, axis=1, keepdims=True))
    alpha = jnp.exp(m2[...] - m_new)
    p = jnp.exp(s - m_new)
    l2[...] = alpha * l2[...] + jnp.sum(p, axis=1, keepdims=True)
    for hh in range(DIFF_HEADS):
        vh = bdv[slot, pl.ds(hh, kc, stride=DIFF_HEADS), :]
        a2[hh] = alpha * a2[hh] + jnp.dot(p, vh, preferred_element_type=F32)
    m2[...] = m_new

    @pl.when(c == nchunk - 1)
    def _():
        ckvn = ckvn_ref[...]
        s_new = (jnp.sum(qa * ckvn, axis=1, keepdims=True)
                 + jnp.sum(qr * krn_ref[...], axis=1, keepdims=True))
        m_f = jnp.maximum(m1[...], s_new)
        al = jnp.exp(m1[...] - m_f)
        pn = jnp.exp(s_new - m_f)
        inv = 1.0 / (al * l1[...] + pn)
        olat_ref[...] = (al * a1[...] + pn * ckvn) * inv
        s_new = jnp.sum(qbd * dkn_ref[...], axis=1, keepdims=True)
        m_f = jnp.maximum(m2[...], s_new)
        al = jnp.exp(m2[...] - m_f)
        pn = jnp.exp(s_new - m_f)
        inv = 1.0 / (al * l2[...] + pn)
        dvn = dvn_ref[...]
        for hh in range(DIFF_HEADS):
            sl = slice(hh * 2 * DIFF_HD, (hh + 1) * 2 * DIFF_HD)
            od_ref[:, sl] = (al * a2[hh] + pn * dvn[:, sl]) * inv


def _decode(page_table, qcat, qd, ckv_new, kr_new, dk_new, dv_new, ckv_c, krT_c, dkT_c, dv_c, pages):
    nreq, npages = page_table.shape
    nchunk = npages // pages
    kc = pages * PAGE
    sq = lambda *tail: pl.BlockSpec((None,) + tail, lambda b, c, pt: (b,) + (0,) * len(tail))
    anyspec = pl.BlockSpec(memory_space=pl.ANY)
    grid_spec = pltpu.PrefetchScalarGridSpec(
        num_scalar_prefetch=1,
        grid=(nreq, nchunk),
        in_specs=[sq(8, 384), sq(1, 512), sq(1, 256), sq(1, MLA_ROPE), sq(1, 512), sq(1, 512),
                  anyspec, anyspec, anyspec, anyspec],
        out_specs=[sq(8, 256), sq(8, 512)],
        scratch_shapes=[pltpu.VMEM((2, kc, 256), F32), pltpu.VMEM((2, MLA_ROPE, kc), F32),
                        pltpu.VMEM((2, 512, kc), F32), pltpu.VMEM((2, 4 * kc, LANES), F32),
                        pltpu.SemaphoreType.DMA((2, 4)),
                        pltpu.VMEM((8, 1), F32), pltpu.VMEM((8, 1), F32), pltpu.VMEM((8, 256), F32),
                        pltpu.VMEM((8, 1), F32), pltpu.VMEM((8, 1), F32), pltpu.VMEM((DIFF_HEADS, 8, LANES), F32)])
    return pl.pallas_call(
        functools.partial(_decode_kernel, pages=pages, nchunk=nchunk, nreq=nreq),
        grid_spec=grid_spec,
        out_shape=[jax.ShapeDtypeStruct((nreq, 8, 256), F32), jax.ShapeDtypeStruct((nreq, 8, 512), F32)],
        compiler_params=_params(("arbitrary", "arbitrary")),
        name="decode",
    )(page_table, qcat, qd, ckv_new, kr_new, dk_new, dv_new, ckv_c, krT_c, dkT_c, dv_c)


def _post_kernel(prompt, x_ref, sc2_ref, sh2_ref, g2_ref, sc3_ref, sh3_ref, g3_ref, om_ref, od_ref,
                 wg_ref, wbm_ref, wbd_ref, wo_ref, subg_ref, dl_ref, lng2_ref, lnb2_ref, lng3_ref, lnb3_ref,
                 w1_ref, w2_ref, wuv_ref, o_ref):
    x = x_ref[...]
    h = (x * (1.0 + sc2_ref[...]) + sh2_ref[...]).astype(BF16)
    gl = jnp.dot(h, wg_ref[...], preferred_element_type=F32)
    gate_m = jax.nn.sigmoid(gl[:, :D_MODEL])
    gate_d = jax.nn.sigmoid(gl[:, D_MODEL:])

    dl = dl_ref[...]
    lam = (jnp.exp(jnp.sum(dl[0:1] * dl[1:2], axis=1, keepdims=True))
           - jnp.exp(jnp.sum(dl[2:3] * dl[3:4], axis=1, keepdims=True)) + LAMBDA_INIT)
    dv = 2 * DIFF_HD
    if prompt:
        o_mla = om_ref[...].astype(F32).T.astype(BF16)
        b_mla = jnp.dot(o_mla, wbm_ref[...], preferred_element_type=F32)
        od = od_ref[...].T
        maps = [od[:, j * dv:(j + 1) * dv] for j in range(2 * DIFF_HEADS)]
    else:
        b_mla = None
        for hh in range(MLA_HEADS):
            o_h = jnp.dot(om_ref[:, hh * MLA_KV_LORA:(hh + 1) * MLA_KV_LORA].astype(BF16), wuv_ref[hh],
                          preferred_element_type=F32)
            part = jnp.dot(o_h.astype(BF16), wbm_ref[hh * MLA_V:(hh + 1) * MLA_V, :], preferred_element_type=F32)
            b_mla = part if b_mla is None else b_mla + part
        maps = [od_ref[:, j * 512 + (j // 2) * dv:j * 512 + (j // 2 + 1) * dv] for j in range(2 * DIFF_HEADS)]
    b_diff = None
    for hh in range(DIFF_HEADS):
        a = maps[2 * hh] - lam * maps[2 * hh + 1]
        a = a * _rms_scale(a) * subg_ref[...] * (1.0 - LAMBDA_INIT)
        part = jnp.dot(a.astype(BF16), wbd_ref[hh * dv:(hh + 1) * dv, :], preferred_element_type=F32)
        b_diff = part if b_diff is None else b_diff + part
    mix = jnp.dot((gate_m * b_mla + gate_d * b_diff).astype(BF16), wo_ref[...], preferred_element_type=F32)
    x2 = _layernorm(ALPHA * x + (1.0 + g2_ref[...]) * mix, lng2_ref[...], lnb2_ref[...])

    xm = (x2 * (1.0 + sc3_ref[...]) + sh3_ref[...]).astype(BF16)
    f = _swiglu(xm, w1_ref, w2_ref)
    o_ref[...] = _layernorm(ALPHA * x2 + 0.5 * (1.0 + g3_ref[...]) * f, lng3_ref[...], lnb3_ref[...])


def _post_prompt_kernel(*refs):
    _post_kernel(True, *refs[:21], None, refs[21])


def _post_sample_kernel(*refs):
    _post_kernel(False, *refs)


def _post(prompt, x, mod, om, od, wg, wbm, wbd, wo, subg, dl, lns, w1, w2, wuv3, tm):
    n = x.shape[0]
    ln_g = lns[0]
    if prompt:
        om_spec = pl.BlockSpec((om.shape[0], tm), lambda i: (0, i))
        od_spec = pl.BlockSpec((od.shape[0], tm), lambda i: (0, i))
    else:
        om_spec = pl.BlockSpec((tm, om.shape[1]), lambda i: (i, 0))
        od_spec = pl.BlockSpec((tm, od.shape[1]), lambda i: (i, 0))
    in_specs = [pl.BlockSpec((tm, D_MODEL), lambda i: (i, 0)),
                _mod_spec(mod, 4, tm), _mod_spec(mod, 3, tm), _mod_spec(mod, 5, tm),
                _mod_spec(mod, 7, tm), _mod_spec(mod, 6, tm), _mod_spec(mod, 8, tm),
                om_spec, od_spec,
                _const_spec(wg.shape), _const_spec(wbm.shape), _const_spec(wbd.shape), _const_spec(wo.shape),
                _const_spec(subg.shape), _const_spec(dl.shape),
                _const_spec(ln_g.shape), _const_spec(ln_g.shape), _const_spec(ln_g.shape), _const_spec(ln_g.shape),
                _const_spec(w1.shape), _const_spec(w2.shape)]
    args = [x, mod, mod, mod, mod, mod, mod, om, od, wg, wbm, wbd, wo, subg, dl, *lns, w1, w2]
    if not prompt:
        in_specs.append(_const_spec(wuv3.shape))
        args.append(wuv3)
    return pl.pallas_call(
        _post_prompt_kernel if prompt else _post_sample_kernel,
        grid=(n // tm,),
        in_specs=in_specs,
        out_specs=pl.BlockSpec((tm, D_MODEL), lambda i: (i, 0)),
        out_shape=jax.ShapeDtypeStruct((n, D_MODEL), F32),
        compiler_params=_params(("arbitrary",)),
        name="post_prompt" if prompt else "post_sample",
    )(*args)


def _pack_weights(w_in, w_uq, w_uk, w_uv):
    o = [0]
    for w in (MLA_Q_LORA, MLA_KV_LORA, MLA_ROPE, 512, 512, 512, D_MODEL, D_MODEL):
        o.append(o[-1] + w)
    w_qlat, w_ckv, w_kr, w_qd, w_kd, w_vd, w_gm, w_gd = (w_in[:, o[i]:o[i + 1]] for i in range(8))
    half = MLA_ROPE // 2
    swap32 = jnp.concatenate([jnp.arange(half, MLA_ROPE), jnp.arange(0, half)])
    w_kr_sw = w_kr[:, swap32]
    col = jnp.arange(512)
    within = col % DIFF_HD
    partner = jnp.where(within < DIFF_ROT // 2, col + DIFF_ROT // 2, col - DIFF_ROT // 2)
    partner = jnp.where(within < DIFF_ROT, partner, col)
    live = (within < DIFF_ROT).astype(w_in.dtype)
    w_qd_sw = w_qd[:, partner] * live
    w_kd_sw = w_kd[:, partner] * live
    wc = jnp.concatenate([w_qlat, w_ckv, w_kr, w_kr_sw, w_kr, w_kr_sw, w_qd, w_qd_sw, w_kd, w_kd_sw, w_vd],
                         axis=1).astype(BF16)
    wg = jnp.concatenate([w_gm, w_gd], axis=1).astype(BF16)
    nope, rope = w_uq[:, :, :MLA_NOPE], w_uq[:, :, MLA_NOPE:]
    wuq = jnp.concatenate([nope, rope, rope[:, :, swap32]], axis=2).reshape(MLA_Q_LORA, MLA_HEADS * LANES).astype(BF16)
    wuk_pad = jnp.pad(w_uk, ((0, 0), (0, 0), (0, LANES - MLA_NOPE)))
    wk_prompt = wuk_pad.reshape(MLA_KV_LORA, MLA_HEADS * LANES).astype(BF16)
    wk_sample = jnp.transpose(wuk_pad, (1, 2, 0)).astype(BF16)
    wuv = w_uv.reshape(MLA_KV_LORA, MLA_HEADS * MLA_V).astype(BF16)
    wuv3 = jnp.transpose(w_uv, (1, 0, 2)).astype(BF16)
    return wc, wg, wuq, wk_prompt, wk_sample, wuv, wuv3


def _inv_rows():
    lane = jnp.arange(LANES)
    inv_m = ROPE_THETA ** (-jnp.arange(0, MLA_ROPE, 2, dtype=F32) / MLA_ROPE)
    inv_d = ROPE_THETA ** (-jnp.arange(0, DIFF_ROT, 2, dtype=F32) / DIFF_ROT)
    invm = inv_m[lane % (MLA_ROPE // 2)][None, :]
    within = lane % DIFF_HD
    invd = jnp.where(within < DIFF_ROT, inv_d[within % (DIFF_ROT // 2)], 0.0)[None, :]
    return invm, invd


def kernel(x_prompt, x_sample, cache_ckv, cache_krope, cache_diff_k, cache_diff_v, page_table, c_prompt, c_sample,
           w_ada, b_ada, ln_g, ln_b, ffn_w1, ffn_w2, w_in, q_norm_g, kv_norm_g, w_uq, w_uk, w_uv, diff_lambda,
           subln_g, w_br_mla, w_br_diff, w_out):
    assert w_ada.shape[0] == DEPTH and x_prompt.shape[0] == 1 and x_sample.shape[1] == 1
    s = x_prompt.shape[1]
    dec = x_sample.shape[0]
    n_pool, page = cache_ckv.shape[1], cache_ckv.shape[2]
    assert page == PAGE
    past_len = page_table.shape[1] * page
    tm = min(512, s)
    tp = min(256, s)
    pages = min(16, page_table.shape[1])

    wc, wg, wuq, wk_p, wk_s, wuv, wuv3 = _pack_weights(w_in[0], w_uq[0], w_uk[0], w_uv[0])
    w1 = ffn_w1[0].astype(BF16)
    w2 = ffn_w2[0].astype(BF16)
    wbm = w_br_mla[0].astype(BF16)
    wbd = w_br_diff[0].astype(BF16)
    wo = w_out[0].astype(BF16)
    lng = [ln_g[0, k:k + 1] for k in range(3)]
    lnb = [ln_b[0, k:k + 1] for k in range(3)]
    lns = (lng[1], lnb[1], lng[2], lnb[2])
    qg, kvg = q_norm_g, kv_norm_g
    dl = diff_lambda[0]
    invm, invd = _inv_rows()

    rows = 1 + dec
    rpad = -rows % 8
    c_all = jnp.concatenate([c_prompt, c_sample, jnp.zeros((rpad, D_MODEL), F32)], axis=0)
    mod = _adaln(c_all, w_ada[0], b_ada)
    mod_p, mod_s = mod[0:1], mod[1:rows]

    xp = x_prompt[0]
    pos_p = jnp.arange(s, dtype=jnp.int32).astype(F32)[:, None]
    x1p = _ffn_ln(xp, mod_p, lng[0], lnb[0], w1[0], w2[0], tm)
    (ckv_p, krT_p, dkT_p, dv_p, qm, km, qd, kd, vTm, vTd) = _mixer_in(
        True, x1p, mod_p, pos_p, invm, invd, wc, qg, kvg, wuq, wk_p, wuv, tm)
    oTm = _attn_mla(qm, km, vTm, tm)
    oTd = _attn_diff(qd, kd, vTd, tm)
    y_p = _post(True, x1p, mod_p, oTm, oTd, wg, wbm, wbd, wo, subln_g, dl, lns, w1[1], w2[1], None, tp)

    xs = x_sample[:, 0, :]
    pos_s = jnp.full((dec, 1), past_len, jnp.int32).astype(F32)
    x1s = _ffn_ln(xs, mod_s, lng[0], lnb[0], w1[0], w2[0], dec)
    (ckv_s, kr_s, dk_s, dv_s, qcat, qd_s) = _mixer_in(
        False, x1s, mod_s, pos_s, invm, invd, wc, qg, kvg, wuq, wk_s, None, dec)
    ckv_c = cache_ckv[0]
    krT_c = jnp.transpose(cache_krope[0], (0, 2, 1))
    dkT_c = jnp.transpose(cache_diff_k[0], (0, 2, 3, 4, 1)).reshape(n_pool, 512, page)
    dv_c = cache_diff_v[0].reshape(n_pool, page * DIFF_HEADS, 2 * DIFF_HD)
    o_lat, o_dd = _decode(page_table, qcat.reshape(dec, 8, 384), qd_s[:, None, :], ckv_s[:, None, :],
                          kr_s[:, None, :], dk_s[:, None, :], dv_s[:, None, :], ckv_c, krT_c, dkT_c, dv_c, pages)
    y_s = _post(False, x1s, mod_s, o_lat.reshape(dec, 8 * 256), o_dd.reshape(dec, 8 * 512), wg, wbm, wbd, wo,
                subln_g, dl, lns, w1[1], w2[1], wuv3, dec)

    new_ckv_p = ckv_p[None, None]
    new_kr_p = krT_p.T[None, None]
    new_dk_p = dkT_p.reshape(DIFF_HEADS, 2, DIFF_HD, s).transpose(3, 0, 1, 2)[None, None]
    new_dv_p = dv_p.reshape(1, 1, s, DIFF_HEADS, 2 * DIFF_HD)
    new_ckv_s = ckv_s[None, :, None, :]
    new_kr_s = kr_s[None, :, None, :]
    new_dk_s = dk_s.reshape(1, dec, 1, DIFF_HEADS, 2, DIFF_HD)
    new_dv_s = dv_s.reshape(1, dec, 1, DIFF_HEADS, 2 * DIFF_HD)
    return (y_p[None], y_s[:, None, :], new_ckv_p, new_kr_p, new_dk_p, new_dv_p,
            new_ckv_s, new_kr_s, new_dk_s, new_dv_s)
```

```python
import functools
import math

import jax
import jax.numpy as jnp
from jax import lax
from jax.experimental import pallas as pl
from jax.experimental.pallas import tpu as pltpu

F32 = jnp.float32
BF16 = jnp.bfloat16

D_MODEL = 1024
D_FF = 2816
N_MOD = 9
MLA_HEADS = 8
MLA_Q_LORA = 384
MLA_KV_LORA = 256
MLA_NOPE = 64
MLA_ROPE = 32
MLA_V = 64
DIFF_HEADS = 4
DIFF_HD = 64
DIFF_ROT = 16
ROPE_THETA = 500000.0
DEPTH = 1
ALPHA = (2 * DEPTH) ** 0.25
LN_EPS = 1e-5
RMS_EPS = 1e-6
MLA_SCALE = (MLA_NOPE + MLA_ROPE) ** -0.5
DIFF_SCALE = DIFF_HD ** -0.5
LAMBDA_INIT = 0.8 - 0.6 * math.exp(-0.3 * 0)
LOG2E = math.log2(math.e)
PAGE = 128
LANES = 128
NEG = -1e30

VMEM_LIMIT = 56 * 1024 * 1024
C_QLAT, C_CKV, C_KR4, C_QD, C_QDSW, C_KD, C_KDSW, C_VD, C_END = (
    0, 384, 640, 768, 1280, 1792, 2304, 2816, 3328)


def _params(sem=None):
    return pltpu.CompilerParams(dimension_semantics=sem, vmem_limit_bytes=VMEM_LIMIT)


def _const_spec(shape):
    nd = len(shape)
    return pl.BlockSpec(shape, lambda *_: (0,) * nd, pipeline_mode=pl.Buffered(1))


def _layernorm(y, g, b):
    mu = jnp.mean(y, axis=-1, keepdims=True)
    d = y - mu
    var = jnp.mean(d * d, axis=-1, keepdims=True)
    return d * lax.rsqrt(var + LN_EPS) * g + b


def _rms_scale(x):
    return lax.rsqrt(jnp.mean(x * x, axis=-1, keepdims=True) + RMS_EPS)


def _swiglu(xm, w1_ref, w2_ref):
    f = w2_ref.shape[0]
    fc = f // 2
    acc = None
    for j in range(2):
        gate = jnp.dot(xm, w1_ref[:, j * fc:(j + 1) * fc], preferred_element_type=F32)
        up = jnp.dot(xm, w1_ref[:, f + j * fc:f + (j + 1) * fc], preferred_element_type=F32)
        act = (gate * jax.nn.sigmoid(gate) * up).astype(BF16)
        part = jnp.dot(act, w2_ref[j * fc:(j + 1) * fc, :], preferred_element_type=F32)
        acc = part if acc is None else acc + part
    return acc


def _adaln_kernel(c_ref, w_ref, b_ref, o_ref):
    c = c_ref[...]
    a = (c * jax.nn.sigmoid(c)).astype(BF16)
    o_ref[...] = jnp.dot(a, w_ref[...].astype(BF16), preferred_element_type=F32) + b_ref[...]


def _adaln(c_all, w_ada, b_ada):
    r = c_all.shape[0]
    n = w_ada.shape[1]
    tn = D_MODEL
    return pl.pallas_call(
        _adaln_kernel,
        grid=(n // tn,),
        in_specs=[pl.BlockSpec((r, D_MODEL), lambda j: (0, 0)),
                  pl.BlockSpec((D_MODEL, tn), lambda j: (0, j)),
                  pl.BlockSpec((1, tn), lambda j: (0, j))],
        out_specs=pl.BlockSpec((r, tn), lambda j: (0, j)),
        out_shape=jax.ShapeDtypeStruct((r, n), F32),
        compiler_params=_params(("arbitrary",)),
        name="adaln",
    )(c_all, w_ada, b_ada)


def _mod_spec(mod, k, tm):
    if mod.shape[0] == 1:
        return pl.BlockSpec((1, D_MODEL), lambda i: (0, k))
    return pl.BlockSpec((tm, D_MODEL), lambda i: (i, k))


def _ffn_ln_kernel(x_ref, sc_ref, sh_ref, g_ref, lng_ref, lnb_ref, w1_ref, w2_ref, o_ref):
    x = x_ref[...]
    xm = (x * (1.0 + sc_ref[...]) + sh_ref[...]).astype(BF16)
    f = _swiglu(xm, w1_ref, w2_ref)
    y = ALPHA * x + 0.5 * (1.0 + g_ref[...]) * f
    o_ref[...] = _layernorm(y, lng_ref[...], lnb_ref[...])


def _ffn_ln(x, mod, ln_g, ln_b, w1, w2, tm):
    n = x.shape[0]
    return pl.pallas_call(
        _ffn_ln_kernel,
        grid=(n // tm,),
        in_specs=[pl.BlockSpec((tm, D_MODEL), lambda i: (i, 0)),
                  _mod_spec(mod, 1, tm), _mod_spec(mod, 0, tm), _mod_spec(mod, 2, tm),
                  _const_spec(ln_g.shape), _const_spec(ln_b.shape),
                  _const_spec(w1.shape), _const_spec(w2.shape)],
        out_specs=pl.BlockSpec((tm, D_MODEL), lambda i: (i, 0)),
        out_shape=jax.ShapeDtypeStruct((n, D_MODEL), F32),
        compiler_params=_params(("arbitrary",)),
        name="ffn_ln",
    )(x, mod, mod, mod, ln_g, ln_b, w1, w2)


def _rope_tables(pos, invm, invd, tm):
    lane = lax.broadcasted_iota(jnp.int32, (tm, LANES), 1)
    ang = pos * invm
    sgn = jnp.where((lane % 32) < 16, -1.0, 1.0)
    tk = jnp.where((lane // 32) % 2 == 0, jnp.cos(ang), jnp.sin(ang) * sgn)
    angd = pos * invd
    sgnd = jnp.where((lane % 64) < 8, -1.0, 1.0)
    return lane, tk, jnp.cos(angd), jnp.sin(angd) * sgnd


def _mixer_in_body(prompt, x_ref, sc_ref, sh_ref, pos_ref, invm_ref, invd_ref, wc_ref, qg_ref, kvg_ref,
                   wuq_ref, wk_ref, wuv_ref, outs):
    tm = x_ref.shape[0]
    x = x_ref[...]
    h = (x * (1.0 + sc_ref[...]) + sh_ref[...]).astype(BF16)
    proj = jnp.dot(h, wc_ref[...], preferred_element_type=F32)
    q_lat = proj[:, C_QLAT:C_CKV]
    ckv = proj[:, C_CKV:C_KR4]
    kr4 = proj[:, C_KR4:C_QD]
    q_d = proj[:, C_QD:C_QDSW]
    q_dsw = proj[:, C_QDSW:C_KD]
    k_d = proj[:, C_KD:C_KDSW]
    k_dsw = proj[:, C_KDSW:C_VD]
    v_d = proj[:, C_VD:C_END]

    lane, tk, cd, sd = _rope_tables(pos_ref[...], invm_ref[...], invd_ref[...], tm)
    cd4 = jnp.concatenate([cd] * 4, axis=1)
    sd4 = jnp.concatenate([sd] * 4, axis=1)
    qscale = LOG2E if prompt else 1.0
    q_d_rot = (q_d * cd4 + q_dsw * sd4) * (DIFF_SCALE * qscale)
    k_d_rot = k_d * cd4 + k_dsw * sd4

    kr_t = kr4 * tk
    kr_rot = kr_t + pltpu.roll(kr_t, 96, 1)

    ckv_n = ckv * _rms_scale(ckv) * kvg_ref[...]
    ckv_b = ckv_n.astype(BF16)
    qn = (q_lat * _rms_scale(q_lat) * qg_ref[...]).astype(BF16)
    tq = jnp.where(lane < MLA_NOPE, MLA_SCALE * qscale, tk * (MLA_SCALE * qscale))
    q_full = jnp.dot(qn, wuq_ref[...], preferred_element_type=F32) * jnp.concatenate([tq] * MLA_HEADS, axis=1)

    if prompt:
        (ckv_o, krT_o, dkT_o, dv_o, qm_o, km_o, qd_o, kd_o, vTm_o, vTd_o) = outs
        ckv_o[...] = ckv_n
        krT_o[...] = kr_rot.T[:MLA_ROPE, :]
        dkT_o[...] = k_d_rot.T
        dv_o[...] = v_d
        qm_o[...] = q_full.astype(BF16)
        kr_hi = jnp.where(lane >= MLA_NOPE, kr_rot, 0.0)
        k_nope = jnp.dot(ckv_b, wk_ref[...], preferred_element_type=F32)
        km_o[...] = (k_nope + jnp.concatenate([kr_hi] * MLA_HEADS, axis=1)).astype(BF16)
        qd_o[...] = q_d_rot.astype(BF16)
        kd_o[...] = k_d_rot.astype(BF16)
        v_mla = jnp.dot(ckv_b, wuv_ref[...], preferred_element_type=F32)
        vTm_o[0] = v_mla.T.astype(BF16)
        vTd_o[0] = v_d.T.astype(BF16)
    else:
        (ckv_o, kr_o, dk_o, dv_o, qcat_o, qd_o) = outs
        ckv_o[...] = ckv_n
        kr_o[...] = kr_rot[:, :MLA_ROPE]
        dk_o[...] = k_d_rot
        dv_o[...] = v_d
        qd_o[...] = q_d_rot
        for hh in range(MLA_HEADS):
            blk = q_full[:, hh * LANES:(hh + 1) * LANES]
            q_abs = jnp.dot(blk.astype(BF16), wk_ref[hh], preferred_element_type=F32)
            r = blk + pltpu.roll(blk, 96, 1)
            q_rot = jnp.where(lane < MLA_ROPE, pltpu.roll(r, 64, 1), 0.0)
            qcat_o[:, hh * 384:hh * 384 + 256] = q_abs
            qcat_o[:, hh * 384 + 256:(hh + 1) * 384] = q_rot


def _mixer_in_prompt_kernel(x_ref, sc_ref, sh_ref, pos_ref, invm_ref, invd_ref, wc_ref, qg_ref, kvg_ref,
                            wuq_ref, wk_ref, wuv_ref, *outs):
    _mixer_in_body(True, x_ref, sc_ref, sh_ref, pos_ref, invm_ref, invd_ref, wc_ref, qg_ref, kvg_ref,
                   wuq_ref, wk_ref, wuv_ref, outs)


def _mixer_in_sample_kernel(x_ref, sc_ref, sh_ref, pos_ref, invm_ref, invd_ref, wc_ref, qg_ref, kvg_ref,
                            wuq_ref, wk_ref, *outs):
    _mixer_in_body(False, x_ref, sc_ref, sh_ref, pos_ref, invm_ref, invd_ref, wc_ref, qg_ref, kvg_ref,
                   wuq_ref, wk_ref, None, outs)


def _mixer_in(prompt, x, mod, pos, invm, invd, wc, qg, kvg, wuq, wk, wuv, tm):
    n = x.shape[0]
    nt = n // tm
    row = lambda w: pl.BlockSpec((tm, w), lambda i: (i, 0))
    colT = lambda r: pl.BlockSpec((r, tm), lambda i: (0, i))
    in_specs = [row(D_MODEL), _mod_spec(mod, 4, tm), _mod_spec(mod, 3, tm),
                pl.BlockSpec((tm, 1), lambda i: (i, 0)),
                _const_spec(invm.shape), _const_spec(invd.shape), _const_spec(wc.shape),
                _const_spec(qg.shape), _const_spec(kvg.shape), _const_spec(wuq.shape), _const_spec(wk.shape)]
    args = [x, mod, mod, pos, invm, invd, wc, qg, kvg, wuq, wk]
    sds = jax.ShapeDtypeStruct
    if prompt:
        in_specs.append(_const_spec(wuv.shape))
        args.append(wuv)
        out_specs = [row(256), colT(MLA_ROPE), colT(512), row(512), row(1024), row(1024), row(512), row(512),
                     pl.BlockSpec((1, 512, tm), lambda i: (i, 0, 0)),
                     pl.BlockSpec((1, 512, tm), lambda i: (i, 0, 0))]
        out_shape = [sds((n, 256), F32), sds((MLA_ROPE, n), F32), sds((512, n), F32), sds((n, 512), F32),
                     sds((n, 1024), BF16), sds((n, 1024), BF16), sds((n, 512), BF16), sds((n, 512), BF16),
                     sds((nt, 512, tm), BF16), sds((nt, 512, tm), BF16)]
        body = _mixer_in_prompt_kernel
    else:
        out_specs = [row(256), row(MLA_ROPE), row(512), row(512), row(MLA_HEADS * 384), row(512)]
        out_shape = [sds((n, 256), F32), sds((n, MLA_ROPE), F32), sds((n, 512), F32), sds((n, 512), F32),
                     sds((n, MLA_HEADS * 384), F32), sds((n, 512), F32)]
        body = _mixer_in_sample_kernel
    return pl.pallas_call(
        body, grid=(nt,), in_specs=in_specs, out_specs=out_specs, out_shape=out_shape,
        compiler_params=_params(("arbitrary",)),
        name="mixer_in_prompt" if prompt else "mixer_in_sample",
    )(*args)


def _attn_kernel(q_ref, k_ref, vT_ref, o_ref, s0, s1, p0, p1, acc_ref, *, blk, dv, half_axis):
    qi = pl.program_id(half_axis + 1 if half_axis is not None else 1)
    q = q_ref[...]
    if half_axis is not None:
        lane = lax.broadcasted_iota(jnp.int32, q.shape, 1)
        q = jnp.where((lane // DIFF_HD) == pl.program_id(half_axis), q, jnp.zeros_like(q))

    def qk(kb, s_ref):
        start = pl.multiple_of(kb * blk, blk)
        s = lax.dot_general(k_ref[pl.ds(start, blk), :], q, (((1,), (1,)), ((), ())), preferred_element_type=F32)
        s_ref[...] = s
        return jnp.max(s, axis=0, keepdims=True)

    def sm(s, smax, m, l, p_ref):
        m_new = jnp.maximum(m, smax)
        alpha = jnp.exp2(m - m_new)
        p = jnp.exp2(s - m_new)
        p_ref[...] = p.astype(BF16)
        return m_new, alpha * l + jnp.sum(p, axis=0, keepdims=True), alpha

    def sm_diag(s_ref, m, l, p_ref):
        key = lax.broadcasted_iota(jnp.int32, (blk, blk), 0)
        qry = lax.broadcasted_iota(jnp.int32, (blk, blk), 1)
        s = jnp.where(key <= qry, s_ref[...], NEG)
        return sm(s, jnp.max(s, axis=0, keepdims=True), m, l, p_ref)

    def pv(p_ref, kb, alpha):
        acc_ref[...] = alpha * acc_ref[...] + jnp.dot(vT_ref[kb], p_ref[...], preferred_element_type=F32)

    p1[...] = jnp.zeros(p1.shape, BF16)
    acc_ref[...] = jnp.zeros(acc_ref.shape, F32)
    pairs = qi // 2

    def body(j, carry):
        smax0, a1, m, l = carry
        pv(p1, jnp.maximum(2 * j - 1, 0), a1)
        smax1 = qk(2 * j + 1, s1)
        m, l, a0 = sm(s0[...], smax0, m, l, p0)
        pv(p0, 2 * j, a0)
        smax0 = qk(2 * j + 2, s0)
        m, l, a1 = sm(s1[...], smax1, m, l, p1)
        return smax0, a1, m, l

    init = (qk(0, s0), jnp.ones((1, blk), F32), jnp.full((1, blk), NEG, F32), jnp.zeros((1, blk), F32))
    smax0, a1, m, l = lax.fori_loop(0, pairs, body, init)
    pv(p1, jnp.maximum(2 * pairs - 1, 0), a1)

    def tail_odd(smax0, m, l):
        qk(qi, s1)
        m, l, a = sm(s0[...], smax0, m, l, p0)
        pv(p0, qi - 1, a)
        m, l, a = sm_diag(s1, m, l, p1)
        pv(p1, qi, a)
        return l

    def tail_even(smax0, m, l):
        m, l, a = sm_diag(s0, m, l, p0)
        pv(p0, qi, a)
        return l

    l = lax.cond(qi % 2 == 1, tail_odd, tail_even, smax0, m, l)
    o_ref[...] = (acc_ref[...] / l).astype(o_ref.dtype)


def _attn_scratch(blk, dv):
    return [pltpu.VMEM((blk, blk), F32), pltpu.VMEM((blk, blk), F32), pltpu.VMEM((blk, blk), BF16),
            pltpu.VMEM((blk, blk), BF16), pltpu.VMEM((dv, blk), F32)]


def _attn_mla(q, k, vT, blk):
    s = q.shape[0]
    nb = s // blk
    return pl.pallas_call(
        functools.partial(_attn_kernel, blk=blk, dv=MLA_V, half_axis=None),
        grid=(MLA_HEADS, nb),
        in_specs=[pl.BlockSpec((blk, LANES), lambda h, i: (i, h)),
                  pl.BlockSpec((s, LANES), lambda h, i: (0, h)),
                  pl.BlockSpec((nb, MLA_V, blk), lambda h, i: (0, h, 0))],
        out_specs=pl.BlockSpec((MLA_V, blk), lambda h, i: (h, i)),
        out_shape=jax.ShapeDtypeStruct((MLA_HEADS * MLA_V, s), BF16),
        scratch_shapes=_attn_scratch(blk, MLA_V),
        compiler_params=_params(("arbitrary", "arbitrary")),
        name="attn_mla",
    )(q, k, vT)


def _attn_diff(q, k, vT, blk):
    s = q.shape[0]
    nb = s // blk
    dv = 2 * DIFF_HD
    return pl.pallas_call(
        functools.partial(_attn_kernel, blk=blk, dv=dv, half_axis=1),
        grid=(DIFF_HEADS, 2, nb),
        in_specs=[pl.BlockSpec((blk, LANES), lambda h, j, i: (i, h)),
                  pl.BlockSpec((s, LANES), lambda h, j, i: (0, h)),
                  pl.BlockSpec((nb, dv, blk), lambda h, j, i: (0, h, 0))],
        out_specs=pl.BlockSpec((dv, blk), lambda h, j, i: (2 * h + j, i)),
        out_shape=jax.ShapeDtypeStruct((DIFF_HEADS * 2 * dv, s), F32),
        scratch_shapes=_attn_scratch(blk, dv),
        compiler_params=_params(("arbitrary", "arbitrary", "arbitrary")),
        name="attn_diff",
    )(q, k, vT)


def _decode_kernel(pt_ref, q_ref, qd_ref, ckvn_ref, krn_ref, dkn_ref, dvn_ref,
                   ckv_hbm, krT_hbm, dkT_hbm, dv_hbm, olat_ref, od_ref,
                   bckv, bkr, bdk, bdv, sem, m1, l1, a1, m2, l2, a2, *, pages, nchunk, nreq):
    b = pl.program_id(0)
    c = pl.program_id(1)
    t = b * nchunk + c
    slot = t % 2
    kc = pages * PAGE

    def copies(bb, cc, sl):
        out = []
        for g in range(pages):
            pid = pt_ref[bb, cc * pages + g]
            out.append(pltpu.make_async_copy(ckv_hbm.at[pid], bckv.at[sl, pl.ds(g * PAGE, PAGE), :], sem.at[sl, 0]))
            out.append(pltpu.make_async_copy(krT_hbm.at[pid], bkr.at[sl, :, pl.ds(g * PAGE, PAGE)], sem.at[sl, 1]))
            out.append(pltpu.make_async_copy(dkT_hbm.at[pid], bdk.at[sl, :, pl.ds(g * PAGE, PAGE)], sem.at[sl, 2]))
            out.append(pltpu.make_async_copy(dv_hbm.at[pid], bdv.at[sl, pl.ds(g * 4 * PAGE, 4 * PAGE), :],
                                             sem.at[sl, 3]))
        return out

    @pl.when(t == 0)
    def _():
        for cp in copies(0, 0, 0):
            cp.start()

    @pl.when(t + 1 < nreq * nchunk)
    def _():
        wrap = c + 1 == nchunk
        nb = jnp.where(wrap, b + 1, b)
        nc = jnp.where(wrap, 0, c + 1)
        for cp in copies(nb, nc, 1 - slot):
            cp.start()

    @pl.when(c == 0)
    def _():
        m1[...] = jnp.full(m1.shape, NEG, F32)
        l1[...] = jnp.zeros(l1.shape, F32)
        a1[...] = jnp.zeros(a1.shape, F32)
        m2[...] = jnp.full(m2.shape, NEG, F32)
        l2[...] = jnp.zeros(l2.shape, F32)
        a2[...] = jnp.zeros(a2.shape, F32)

    for cp in copies(b, c, slot):
        cp.wait()

    q = q_ref[...]
    qa = q[:, :MLA_KV_LORA]
    qr = q[:, MLA_KV_LORA:MLA_KV_LORA + MLA_ROPE]
    row = lax.broadcasted_iota(jnp.int32, (8, 512), 0)
    blkid = lax.broadcasted_iota(jnp.int32, (8, 512), 1) // DIFF_HD
    qbd = jnp.where(row == blkid, jnp.broadcast_to(qd_ref[...], (8, 512)), 0.0)

    ckv = bckv[slot]
    s = lax.dot_general(qa, ckv, (((1,), (1,)), ((), ())), preferred_element_type=F32)
    s = s + jnp.dot(qr, bkr[slot], preferred_element_type=F32)
    m_new = jnp.maximum(m1[...], jnp.max(s, axis=1, keepdims=True))
    alpha = jnp.exp(m1[...] - m_new)
    p = jnp.exp(s - m_new)
    l1[...] = alpha * l1[...] + jnp.sum(p, axis=1, keepdims=True)
    a1[...] = alpha * a1[...] + jnp.dot(p, ckv, preferred_element_type=F32)
    m1[...] = m_new

    s = jnp.dot(qbd, bdk[slot], preferred_element_type=F32)
    m_new = jnp.maximum(m2[...], jnp.max(s, axis=1, keepdims=True))
    alpha = jnp.exp(m2[...] - m_new)
    p = jnp.exp(s - m_new)
    l2[...] = alpha * l2[...] + jnp.sum(p, axis=1, keepdims=True)
    for hh in range(DIFF_HEADS):
        vh = bdv[slot, pl.ds(hh, kc, stride=DIFF_HEADS), :]
        a2[hh] = alpha * a2[hh] + jnp.dot(p, vh, preferred_element_type=F32)
    m2[...] = m_new

    @pl.when(c == nchunk - 1)
    def _():
        ckvn = ckvn_ref[...]
        s_new = (jnp.sum(qa * ckvn, axis=1, keepdims=True)
                 + jnp.sum(qr * krn_ref[...], axis=1, keepdims=True))
        m_f = jnp.maximum(m1[...], s_new)
        al = jnp.exp(m1[...] - m_f)
        pn = jnp.exp(s_new - m_f)
        inv = 1.0 / (al * l1[...] + pn)
        olat_ref[...] = (al * a1[...] + pn * ckvn) * inv
        s_new = jnp.sum(qbd * dkn_ref[...], axis=1, keepdims=True)
        m_f = jnp.maximum(m2[...], s_new)
        al = jnp.exp(m2[...] - m_f)
        pn = jnp.exp(s_new - m_f)
        inv = 1.0 / (al * l2[...] + pn)
        dvn = dvn_ref[...]
        for hh in range(DIFF_HEADS):
            sl = slice(hh * 2 * DIFF_HD, (hh + 1) * 2 * DIFF_HD)
            od_ref[:, sl] = (al * a2[hh] + pn * dvn[:, sl]) * inv


def _decode(page_table, qcat, qd, ckv_new, kr_new, dk_new, dv_new, ckv_c, krT_c, dkT_c, dv_c, pages):
    nreq, npages = page_table.shape
    nchunk = npages // pages
    kc = pages * PAGE
    sq = lambda *tail: pl.BlockSpec((None,) + tail, lambda b, c, pt: (b,) + (0,) * len(tail))
    anyspec = pl.BlockSpec(memory_space=pl.ANY)
    grid_spec = pltpu.PrefetchScalarGridSpec(
        num_scalar_prefetch=1,
        grid=(nreq, nchunk),
        in_specs=[sq(8, 384), sq(1, 512), sq(1, 256), sq(1, MLA_ROPE), sq(1, 512), sq(1, 512),
                  anyspec, anyspec, anyspec, anyspec],
        out_specs=[sq(8, 256), sq(8, 512)],
        scratch_shapes=[pltpu.VMEM((2, kc, 256), F32), pltpu.VMEM((2, MLA_ROPE, kc), F32),
                        pltpu.VMEM((2, 512, kc), F32), pltpu.VMEM((2, 4 * kc, LANES), F32),
                        pltpu.SemaphoreType.DMA((2, 4)),
                        pltpu.VMEM((8, 1), F32), pltpu.VMEM((8, 1), F32), pltpu.VMEM((8, 256), F32),
                        pltpu.VMEM((8, 1), F32), pltpu.VMEM((8, 1), F32), pltpu.VMEM((DIFF_HEADS, 8, LANES), F32)])
    return pl.pallas_call(
        functools.partial(_decode_kernel, pages=pages, nchunk=nchunk, nreq=nreq),
        grid_spec=grid_spec,
        out_shape=[jax.ShapeDtypeStruct((nreq, 8, 256), F32), jax.ShapeDtypeStruct((nreq, 8, 512), F32)],
        compiler_params=_params(("arbitrary", "arbitrary")),
        name="decode",
    )(page_table, qcat, qd, ckv_new, kr_new, dk_new, dv_new, ckv_c, krT_c, dkT_c, dv_c)


def _post_kernel(prompt, x_ref, sc2_ref, sh2_ref, g2_ref, sc3_ref, sh3_ref, g3_ref, om_ref, od_ref,
                 wg_ref, wbm_ref, wbd_ref, wo_ref, subg_ref, dl_ref, lng2_ref, lnb2_ref, lng3_ref, lnb3_ref,
                 w1_ref, w2_ref, wuv_ref, o_ref):
    x = x_ref[...]
    h = (x * (1.0 + sc2_ref[...]) + sh2_ref[...]).astype(BF16)
    gl = jnp.dot(h, wg_ref[...], preferred_element_type=F32)
    gate_m = jax.nn.sigmoid(gl[:, :D_MODEL])
    gate_d = jax.nn.sigmoid(gl[:, D_MODEL:])

    dl = dl_ref[...]
    lam = (jnp.exp(jnp.sum(dl[0:1] * dl[1:2], axis=1, keepdims=True))
           - jnp.exp(jnp.sum(dl[2:3] * dl[3:4], axis=1, keepdims=True)) + LAMBDA_INIT)
    dv = 2 * DIFF_HD
    if prompt:
        o_mla = om_ref[...].astype(F32).T.astype(BF16)
        b_mla = jnp.dot(o_mla, wbm_ref[...], preferred_element_type=F32)
        od = od_ref[...].T
        maps = [od[:, j * dv:(j + 1) * dv] for j in range(2 * DIFF_HEADS)]
    else:
        b_mla = None
        for hh in range(MLA_HEADS):
            o_h = jnp.dot(om_ref[:, hh * MLA_KV_LORA:(hh + 1) * MLA_KV_LORA].astype(BF16), wuv_ref[hh],
                          preferred_element_type=F32)
            part = jnp.dot(o_h.astype(BF16), wbm_ref[hh * MLA_V:(hh + 1) * MLA_V, :], preferred_element_type=F32)
            b_mla = part if b_mla is None else b_mla + part
        maps = [od_ref[:, j * 512 + (j // 2) * dv:j * 512 + (j // 2 + 1) * dv] for j in range(2 * DIFF_HEADS)]
    b_diff = None
    for hh in range(DIFF_HEADS):
        a = maps[2 * hh] - lam * maps[2 * hh + 1]
        a = a * _rms_scale(a) * subg_ref[...] * (1.0 - LAMBDA_INIT)
        part = jnp.dot(a.astype(BF16), wbd_ref[hh * dv:(hh + 1) * dv, :], preferred_element_type=F32)
        b_diff = part if b_diff is None else b_diff + part
    mix = jnp.dot((gate_m * b_mla + gate_d * b_diff).astype(BF16), wo_ref[...], preferred_element_type=F32)
    x2 = _layernorm(ALPHA * x + (1.0 + g2_ref[...]) * mix, lng2_ref[...], lnb2_ref[...])

    xm = (x2 * (1.0 + sc3_ref[...]) + sh3_ref[...]).astype(BF16)
    f = _swiglu(xm, w1_ref, w2_ref)
    o_ref[...] = _layernorm(ALPHA * x2 + 0.5 * (1.0 + g3_ref[...]) * f, lng3_ref[...], lnb3_ref[...])


def _post_prompt_kernel(*refs):
    _post_kernel(True, *refs[:21], None, refs[21])


def _post_sample_kernel(*refs):
    _post_kernel(False, *refs)


def _post(prompt, x, mod, om, od, wg, wbm, wbd, wo, subg, dl, lns, w1, w2, wuv3, tm):
    n = x.shape[0]
    ln_g = lns[0]
    if prompt:
        om_spec = pl.BlockSpec((om.shape[0], tm), lambda i: (0, i))
        od_spec = pl.BlockSpec((od.shape[0], tm), lambda i: (0, i))
    else:
        om_spec = pl.BlockSpec((tm, om.shape[1]), lambda i: (i, 0))
        od_spec = pl.BlockSpec((tm, od.shape[1]), lambda i: (i, 0))
    in_specs = [pl.BlockSpec((tm, D_MODEL), lambda i: (i, 0)),
                _mod_spec(mod, 4, tm), _mod_spec(mod, 3, tm), _mod_spec(mod, 5, tm),
                _mod_spec(mod, 7, tm), _mod_spec(mod, 6, tm), _mod_spec(mod, 8, tm),
                om_spec, od_spec,
                _const_spec(wg.shape), _const_spec(wbm.shape), _const_spec(wbd.shape), _const_spec(wo.shape),
                _const_spec(subg.shape), _const_spec(dl.shape),
                _const_spec(ln_g.shape), _const_spec(ln_g.shape), _const_spec(ln_g.shape), _const_spec(ln_g.shape),
                _const_spec(w1.shape), _const_spec(w2.shape)]
    args = [x, mod, mod, mod, mod, mod, mod, om, od, wg, wbm, wbd, wo, subg, dl, *lns, w1, w2]
    if not prompt:
        in_specs.append(_const_spec(wuv3.shape))
        args.append(wuv3)
    return pl.pallas_call(
        _post_prompt_kernel if prompt else _post_sample_kernel,
        grid=(n // tm,),
        in_specs=in_specs,
        out_specs=pl.BlockSpec((tm, D_MODEL), lambda i: (i, 0)),
        out_shape=jax.ShapeDtypeStruct((n, D_MODEL), F32),
        compiler_params=_params(("arbitrary",)),
        name="post_prompt" if prompt else "post_sample",
    )(*args)


def _pack_weights(w_in, w_uq, w_uk, w_uv):
    o = [0]
    for w in (MLA_Q_LORA, MLA_KV_LORA, MLA_ROPE, 512, 512, 512, D_MODEL, D_MODEL):
        o.append(o[-1] + w)
    w_qlat, w_ckv, w_kr, w_qd, w_kd, w_vd, w_gm, w_gd = (w_in[:, o[i]:o[i + 1]] for i in range(8))
    half = MLA_ROPE // 2
    swap32 = jnp.concatenate([jnp.arange(half, MLA_ROPE), jnp.arange(0, half)])
    w_kr_sw = w_kr[:, swap32]
    col = jnp.arange(512)
    within = col % DIFF_HD
    partner = jnp.where(within < DIFF_ROT // 2, col + DIFF_ROT // 2, col - DIFF_ROT // 2)
    partner = jnp.where(within < DIFF_ROT, partner, col)
    live = (within < DIFF_ROT).astype(w_in.dtype)
    w_qd_sw = w_qd[:, partner] * live
    w_kd_sw = w_kd[:, partner] * live
    wc = jnp.concatenate([w_qlat, w_ckv, w_kr, w_kr_sw, w_kr, w_kr_sw, w_qd, w_qd_sw, w_kd, w_kd_sw, w_vd],
                         axis=1).astype(BF16)
    wg = jnp.concatenate([w_gm, w_gd], axis=1).astype(BF16)
    nope, rope = w_uq[:, :, :MLA_NOPE], w_uq[:, :, MLA_NOPE:]
    wuq = jnp.concatenate([nope, rope, rope[:, :, swap32]], axis=2).reshape(MLA_Q_LORA, MLA_HEADS * LANES).astype(BF16)
    wuk_pad = jnp.pad(w_uk, ((0, 0), (0, 0), (0, LANES - MLA_NOPE)))
    wk_prompt = wuk_pad.reshape(MLA_KV_LORA, MLA_HEADS * LANES).astype(BF16)
    wk_sample = jnp.transpose(wuk_pad, (1, 2, 0)).astype(BF16)
    wuv = w_uv.reshape(MLA_KV_LORA, MLA_HEADS * MLA_V).astype(BF16)
    wuv3 = jnp.transpose(w_uv, (1, 0, 2)).astype(BF16)
    return wc, wg, wuq, wk_prompt, wk_sample, wuv, wuv3


def _inv_rows():
    lane = jnp.arange(LANES)
    inv_m = ROPE_THETA ** (-jnp.arange(0, MLA_ROPE, 2, dtype=F32) / MLA_ROPE)
    inv_d = ROPE_THETA ** (-jnp.arange(0, DIFF_ROT, 2, dtype=F32) / DIFF_ROT)
    invm = inv_m[lane % (MLA_ROPE // 2)][None, :]
    within = lane % DIFF_HD
    invd = jnp.where(within < DIFF_ROT, inv_d[within % (DIFF_ROT // 2)], 0.0)[None, :]
    return invm, invd


def kernel(x_prompt, x_sample, cache_ckv, cache_krope, cache_diff_k, cache_diff_v, page_table, c_prompt, c_sample,
           w_ada, b_ada, ln_g, ln_b, ffn_w1, ffn_w2, w_in, q_norm_g, kv_norm_g, w_uq, w_uk, w_uv, diff_lambda,
           subln_g, w_br_mla, w_br_diff, w_out):
    assert w_ada.shape[0] == DEPTH and x_prompt.shape[0] == 1 and x_sample.shape[1] == 1
    s = x_prompt.shape[1]
    dec = x_sample.shape[0]
    n_pool, page = cache_ckv.shape[1], cache_ckv.shape[2]
    assert page == PAGE
    past_len = page_table.shape[1] * page
    tm = min(512, s)
    tp = min(256, s)
    pages = min(16, page_table.shape[1])

    wc, wg, wuq, wk_p, wk_s, wuv, wuv3 = _pack_weights(w_in[0], w_uq[0], w_uk[0], w_uv[0])
    w1 = ffn_w1[0].astype(BF16)
    w2 = ffn_w2[0].astype(BF16)
    wbm = w_br_mla[0].astype(BF16)
    wbd = w_br_diff[0].astype(BF16)
    wo = w_out[0].astype(BF16)
    lng = [ln_g[0, k:k + 1] for k in range(3)]
    lnb = [ln_b[0, k:k + 1] for k in range(3)]
    lns = (lng[1], lnb[1], lng[2], lnb[2])
    qg, kvg = q_norm_g, kv_norm_g
    dl = diff_lambda[0]
    invm, invd = _inv_rows()

    rows = 1 + dec
    rpad = -rows % 8
    c_all = jnp.concatenate([c_prompt, c_sample, jnp.zeros((rpad, D_MODEL), F32)], axis=0)
    mod = _adaln(c_all, w_ada[0], b_ada)
    mod_p, mod_s = mod[0:1], mod[1:rows]

    xp = x_prompt[0]
    pos_p = jnp.arange(s, dtype=jnp.int32).astype(F32)[:, None]
    x1p = _ffn_ln(xp, mod_p, lng[0], lnb[0], w1[0], w2[0], tm)
    (ckv_p, krT_p, dkT_p, dv_p, qm, km, qd, kd, vTm, vTd) = _mixer_in(
        True, x1p, mod_p, pos_p, invm, invd, wc, qg, kvg, wuq, wk_p, wuv, tm)
    oTm = _attn_mla(qm, km, vTm, tm)
    oTd = _attn_diff(qd, kd, vTd, tm)
    y_p = _post(True, x1p, mod_p, oTm, oTd, wg, wbm, wbd, wo, subln_g, dl, lns, w1[1], w2[1], None, tp)

    xs = x_sample[:, 0, :]
    pos_s = jnp.full((dec, 1), past_len, jnp.int32).astype(F32)
    x1s = _ffn_ln(xs, mod_s, lng[0], lnb[0], w1[0], w2[0], dec)
    (ckv_s, kr_s, dk_s, dv_s, qcat, qd_s) = _mixer_in(
        False, x1s, mod_s, pos_s, invm, invd, wc, qg, kvg, wuq, wk_s, None, dec)
    ckv_c = cache_ckv[0]
    krT_c = jnp.transpose(cache_krope[0], (0, 2, 1))
    dkT_c = jnp.transpose(cache_diff_k[0], (0, 2, 3, 4, 1)).reshape(n_pool, 512, page)
    dv_c = cache_diff_v[0].reshape(n_pool, page * DIFF_HEADS, 2 * DIFF_HD)
    o_lat, o_dd = _decode(page_table, qcat.reshape(dec, 8, 384), qd_s[:, None, :], ckv_s[:, None, :],
                          kr_s[:, None, :], dk_s[:, None, :], dv_s[:, None, :], ckv_c, krT_c, dkT_c, dv_c, pages)
    y_s = _post(False, x1s, mod_s, o_lat.reshape(dec, 8 * 256), o_dd.reshape(dec, 8 * 512), wg, wbm, wbd, wo,
                subln_g, dl, lns, w1[1], w2[1], wuv3, dec)

    new_ckv_p = ckv_p[None, None]
    new_kr_p = krT_p.T[None, None]
    new_dk_p = dkT_p.reshape(DIFF_HEADS, 2, DIFF_HD, s).transpose(3, 0, 1, 2)[None, None]
    new_dv_p = dv_p.reshape(1, 1, s, DIFF_HEADS, 2 * DIFF_HD)
    new_ckv_s = ckv_s[None, :, None, :]
    new_kr_s = kr_s[None, :, None, :]
    new_dk_s = dk_s.reshape(1, dec, 1, DIFF_HEADS, 2, DIFF_HD)
    new_dv_s = dv_s.reshape(1, dec, 1, DIFF_HEADS, 2 * DIFF_HD)
    return (y_p[None], y_s[:, None, :], new_ckv_p, new_kr_p, new_dk_p, new_dv_p,
            new_ckv_s, new_kr_s, new_dk_s, new_dv_s)
```

```python
import functools
import math

import jax
import jax.numpy as jnp
from jax import lax
from jax.experimental import pallas as pl
from jax.experimental.pallas import tpu as pltpu

F32 = jnp.float32
BF16 = jnp.bfloat16

D_MODEL = 1024
D_FF = 2816
N_MOD = 9
MLA_HEADS = 8
MLA_Q_LORA = 384
MLA_KV_LORA = 256
MLA_NOPE = 64
MLA_ROPE = 32
MLA_V = 64
DIFF_HEADS = 4
DIFF_HD = 64
DIFF_ROT = 16
ROPE_THETA = 500000.0
DEPTH = 1
ALPHA = (2 * DEPTH) ** 0.25
LN_EPS = 1e-5
RMS_EPS = 1e-6
MLA_SCALE = (MLA_NOPE + MLA_ROPE) ** -0.5
DIFF_SCALE = DIFF_HD ** -0.5
LAMBDA_INIT = 0.8 - 0.6 * math.exp(-0.3 * 0)
LOG2E = math.log2(math.e)
PAGE = 128
LANES = 128
NEG = -1e30

VMEM_LIMIT = 56 * 1024 * 1024
C_QLAT, C_CKV, C_KR4, C_QD, C_QDSW, C_KD, C_KDSW, C_VD, C_END = (
    0, 384, 640, 768, 1280, 1792, 2304, 2816, 3328)


def _params(sem=None):
    return pltpu.CompilerParams(dimension_semantics=sem, vmem_limit_bytes=VMEM_LIMIT)


def _const_spec(shape):
    nd = len(shape)
    return pl.BlockSpec(shape, lambda *_: (0,) * nd, pipeline_mode=pl.Buffered(1))


def _layernorm(y, g, b):
    mu = jnp.mean(y, axis=-1, keepdims=True)
    d = y - mu
    var = jnp.mean(d * d, axis=-1, keepdims=True)
    return d * lax.rsqrt(var + LN_EPS) * g + b


def _rms_scale(x):
    return lax.rsqrt(jnp.mean(x * x, axis=-1, keepdims=True) + RMS_EPS)


def _swiglu(xm, w1_ref, w2_ref):
    f = w2_ref.shape[0]
    fc = f // 2
    acc = None
    for j in range(2):
        gate = jnp.dot(xm, w1_ref[:, j * fc:(j + 1) * fc], preferred_element_type=F32)
        up = jnp.dot(xm, w1_ref[:, f + j * fc:f + (j + 1) * fc], preferred_element_type=F32)
        act = (gate * jax.nn.sigmoid(gate) * up).astype(BF16)
        part = jnp.dot(act, w2_ref[j * fc:(j + 1) * fc, :], preferred_element_type=F32)
        acc = part if acc is None else acc + part
    return acc


def _adaln_kernel(c_ref, w_ref, b_ref, o_ref):
    c = c_ref[...]
    a = (c * jax.nn.sigmoid(c)).astype(BF16)
    o_ref[...] = jnp.dot(a, w_ref[...].astype(BF16), preferred_element_type=F32) + b_ref[...]


def _adaln(c_all, w_ada, b_ada):
    r = c_all.shape[0]
    n = w_ada.shape[1]
    tn = D_MODEL
    return pl.pallas_call(
        _adaln_kernel,
        grid=(n // tn,),
        in_specs=[pl.BlockSpec((r, D_MODEL), lambda j: (0, 0)),
                  pl.BlockSpec((D_MODEL, tn), lambda j: (0, j)),
                  pl.BlockSpec((1, tn), lambda j: (0, j))],
        out_specs=pl.BlockSpec((r, tn), lambda j: (0, j)),
        out_shape=jax.ShapeDtypeStruct((r, n), F32),
        compiler_params=_params(("arbitrary",)),
        name="adaln",
    )(c_all, w_ada, b_ada)


def _mod_spec(mod, k, tm):
    if mod.shape[0] == 1:
        return pl.BlockSpec((1, D_MODEL), lambda i: (0, k))
    return pl.BlockSpec((tm, D_MODEL), lambda i: (i, k))


def _ffn_ln_kernel(x_ref, sc_ref, sh_ref, g_ref, lng_ref, lnb_ref, w1_ref, w2_ref, o_ref):
    x = x_ref[...]
    xm = (x * (1.0 + sc_ref[...]) + sh_ref[...]).astype(BF16)
    f = _swiglu(xm, w1_ref, w2_ref)
    y = ALPHA * x + 0.5 * (1.0 + g_ref[...]) * f
    o_ref[...] = _layernorm(y, lng_ref[...], lnb_ref[...])


def _ffn_ln(x, mod, ln_g, ln_b, w1, w2, tm):
    n = x.shape[0]
    return pl.pallas_call(
        _ffn_ln_kernel,
        grid=(n // tm,),
        in_specs=[pl.BlockSpec((tm, D_MODEL), lambda i: (i, 0)),
                  _mod_spec(mod, 1, tm), _mod_spec(mod, 0, tm), _mod_spec(mod, 2, tm),
                  _const_spec(ln_g.shape), _const_spec(ln_b.shape),
                  _const_spec(w1.shape), _const_spec(w2.shape)],
        out_specs=pl.BlockSpec((tm, D_MODEL), lambda i: (i, 0)),
        out_shape=jax.ShapeDtypeStruct((n, D_MODEL), F32),
        compiler_params=_params(("arbitrary",)),
        name="ffn_ln",
    )(x, mod, mod, mod, ln_g, ln_b, w1, w2)


def _rope_tables(pos, invm, invd, tm):
    lane = lax.broadcasted_iota(jnp.int32, (tm, LANES), 1)
    ang = pos * invm
    sgn = jnp.where((lane % 32) < 16, -1.0, 1.0)
    tk = jnp.where((lane // 32) % 2 == 0, jnp.cos(ang), jnp.sin(ang) * sgn)
    angd = pos * invd
    sgnd = jnp.where((lane % 64) < 8, -1.0, 1.0)
    return lane, tk, jnp.cos(angd), jnp.sin(angd) * sgnd


def _mixer_in_body(prompt, x_ref, sc_ref, sh_ref, pos_ref, invm_ref, invd_ref, wc_ref, qg_ref, kvg_ref,
                   wuq_ref, wk_ref, wuv_ref, outs):
    tm = x_ref.shape[0]
    x = x_ref[...]
    h = (x * (1.0 + sc_ref[...]) + sh_ref[...]).astype(BF16)
    proj = jnp.dot(h, wc_ref[...], preferred_element_type=F32)
    q_lat = proj[:, C_QLAT:C_CKV]
    ckv = proj[:, C_CKV:C_KR4]
    kr4 = proj[:, C_KR4:C_QD]
    q_d = proj[:, C_QD:C_QDSW]
    q_dsw = proj[:, C_QDSW:C_KD]
    k_d = proj[:, C_KD:C_KDSW]
    k_dsw = proj[:, C_KDSW:C_VD]
    v_d = proj[:, C_VD:C_END]

    lane, tk, cd, sd = _rope_tables(pos_ref[...], invm_ref[...], invd_ref[...], tm)
    cd4 = jnp.concatenate([cd] * 4, axis=1)
    sd4 = jnp.concatenate([sd] * 4, axis=1)
    qscale = LOG2E if prompt else 1.0
    q_d_rot = (q_d * cd4 + q_dsw * sd4) * (DIFF_SCALE * qscale)
    k_d_rot = k_d * cd4 + k_dsw * sd4

    kr_t = kr4 * tk
    kr_rot = kr_t + pltpu.roll(kr_t, 96, 1)

    ckv_n = ckv * _rms_scale(ckv) * kvg_ref[...]
    ckv_b = ckv_n.astype(BF16)
    qn = (q_lat * _rms_scale(q_lat) * qg_ref[...]).astype(BF16)
    tq = jnp.where(lane < MLA_NOPE, MLA_SCALE * qscale, tk * (MLA_SCALE * qscale))
    q_full = jnp.dot(qn, wuq_ref[...], preferred_element_type=F32) * jnp.concatenate([tq] * MLA_HEADS, axis=1)

    if prompt:
        (ckv_o, krT_o, dkT_o, dv_o, qm_o, km_o, qd_o, kd_o, vTm_o, vTd_o) = outs
        ckv_o[...] = ckv_n
        krT_o[...] = kr_rot.T[:MLA_ROPE, :]
        dkT_o[...] = k_d_rot.T
        dv_o[...] = v_d
        qm_o[...] = q_full.astype(BF16)
        kr_hi = jnp.where(lane >= MLA_NOPE, kr_rot, 0.0)
        k_nope = jnp.dot(ckv_b, wk_ref[...], preferred_element_type=F32)
        km_o[...] = (k_nope + jnp.concatenate([kr_hi] * MLA_HEADS, axis=1)).astype(BF16)
        qd_o[...] = q_d_rot.astype(BF16)
        kd_o[...] = k_d_rot.astype(BF16)
        v_mla = jnp.dot(ckv_b, wuv_ref[...], preferred_element_type=F32)
        vTm_o[0] = v_mla.T.astype(BF16)
        vTd_o[0] = v_d.T.astype(BF16)
    else:
        (ckv_o, kr_o, dk_o, dv_o, qcat_o, qd_o) = outs
        ckv_o[...] = ckv_n
        kr_o[...] = kr_rot[:, :MLA_ROPE]
        dk_o[...] = k_d_rot
        dv_o[...] = v_d
        qd_o[...] = q_d_rot
        for hh in range(MLA_HEADS):
            blk = q_full[:, hh * LANES:(hh + 1) * LANES]
            q_abs = jnp.dot(blk.astype(BF16), wk_ref[hh], preferred_element_type=F32)
            r = blk + pltpu.roll(blk, 96, 1)
            q_rot = jnp.where(lane < MLA_ROPE, pltpu.roll(r, 64, 1), 0.0)
            qcat_o[:, hh * 384:hh * 384 + 256] = q_abs
            qcat_o[:, hh * 384 + 256:(hh + 1) * 384] = q_rot


def _mixer_in_prompt_kernel(x_ref, sc_ref, sh_ref, pos_ref, invm_ref, invd_ref, wc_ref, qg_ref, kvg_ref,
                            wuq_ref, wk_ref, wuv_ref, *outs):
    _mixer_in_body(True, x_ref, sc_ref, sh_ref, pos_ref, invm_ref, invd_ref, wc_ref, qg_ref, kvg_ref,
                   wuq_ref, wk_ref, wuv_ref, outs)


def _mixer_in_sample_kernel(x_ref, sc_ref, sh_ref, pos_ref, invm_ref, invd_ref, wc_ref, qg_ref, kvg_ref,
                            wuq_ref, wk_ref, *outs):
    _mixer_in_body(False, x_ref, sc_ref, sh_ref, pos_ref, invm_ref, invd_ref, wc_ref, qg_ref, kvg_ref,
                   wuq_ref, wk_ref, None, outs)


def _mixer_in(prompt, x, mod, pos, invm, invd, wc, qg, kvg, wuq, wk, wuv, tm):
    n = x.shape[0]
    nt = n // tm
    row = lambda w: pl.BlockSpec((tm, w), lambda i: (i, 0))
    colT = lambda r: pl.BlockSpec((r, tm), lambda i: (0, i))
    in_specs = [row(D_MODEL), _mod_spec(mod, 4, tm), _mod_spec(mod, 3, tm),
                pl.BlockSpec((tm, 1), lambda i: (i, 0)),
                _const_spec(invm.shape), _const_spec(invd.shape), _const_spec(wc.shape),
                _const_spec(qg.shape), _const_spec(kvg.shape), _const_spec(wuq.shape), _const_spec(wk.shape)]
    args = [x, mod, mod, pos, invm, invd, wc, qg, kvg, wuq, wk]
    sds = jax.ShapeDtypeStruct
    if prompt:
        in_specs.append(_const_spec(wuv.shape))
        args.append(wuv)
        out_specs = [row(256), colT(MLA_ROPE), colT(512), row(512), row(1024), row(1024), row(512), row(512),
                     pl.BlockSpec((1, 512, tm), lambda i: (i, 0, 0)),
                     pl.BlockSpec((1, 512, tm), lambda i: (i, 0, 0))]
        out_shape = [sds((n, 256), F32), sds((MLA_ROPE, n), F32), sds((512, n), F32), sds((n, 512), F32),
                     sds((n, 1024), BF16), sds((n, 1024), BF16), sds((n, 512), BF16), sds((n, 512), BF16),
                     sds((nt, 512, tm), BF16), sds((nt, 512, tm), BF16)]
        body = _mixer_in_prompt_kernel
    else:
        out_specs = [row(256), row(MLA_ROPE), row(512), row(512), row(MLA_HEADS * 384), row(512)]
        out_shape = [sds((n, 256), F32), sds((n, MLA_ROPE), F32), sds((n, 512), F32), sds((n, 512), F32),
                     sds((n, MLA_HEADS * 384), F32), sds((n, 512), F32)]
        body = _mixer_in_sample_kernel
    return pl.pallas_call(
        body, grid=(nt,), in_specs=in_specs, out_specs=out_specs, out_shape=out_shape,
        compiler_params=_params(("arbitrary",)),
        name="mixer_in_prompt" if prompt else "mixer_in_sample",
    )(*args)


def _attn_body(q_ref, k_ref, vT_ref, o_ref, s0, s1, p0, p1, acc_ref, *, blk, dv, half_axis):
    qi = pl.program_id(half_axis + 1 if half_axis is not None else 1)
    q = q_ref[...]
    if half_axis is not None:
        lane = lax.broadcasted_iota(jnp.int32, q.shape, 1)
        q = jnp.where((lane // DIFF_HD) == pl.program_id(half_axis), q, jnp.zeros_like(q))

    def qk(kb, s_ref):
        start = pl.multiple_of(kb * blk, blk)
        s = lax.dot_general(k_ref[pl.ds(start, blk), :], q, (((1,), (1,)), ((), ())), preferred_element_type=F32)
        s_ref[...] = s
        return jnp.max(s, axis=0, keepdims=True)

    def sm(s, smax, m, l, p_ref):
        m_new = jnp.maximum(m, smax)
        alpha = jnp.exp2(m - m_new)
        p = jnp.exp2(s - m_new)
        p_ref[...] = p.astype(BF16)
        return m_new, alpha * l + jnp.sum(p, axis=0, keepdims=True), alpha

    def sm_diag(s_ref, m, l, p_ref):
        key = lax.broadcasted_iota(jnp.int32, (blk, blk), 0)
        qry = lax.broadcasted_iota(jnp.int32, (blk, blk), 1)
        s = jnp.where(key <= qry, s_ref[...], NEG)
        return sm(s, jnp.max(s, axis=0, keepdims=True), m, l, p_ref)

    def pv(p_ref, kb, alpha):
        acc_ref[...] = alpha * acc_ref[...] + jnp.dot(vT_ref[kb], p_ref[...], preferred_element_type=F32)

    p1[...] = jnp.zeros(p1.shape, BF16)
    acc_ref[...] = jnp.zeros(acc_ref.shape, F32)
    pairs = qi // 2

    def body(j, carry):
        smax0, a1, m, l = carry
        pv(p1, jnp.maximum(2 * j - 1, 0), a1)
        smax1 = qk(2 * j + 1, s1)
        m, l, a0 = sm(s0[...], smax0, m, l, p0)
        pv(p0, 2 * j, a0)
        smax0 = qk(2 * j + 2, s0)
        m, l, a1 = sm(s1[...], smax1, m, l, p1)
        return smax0, a1, m, l

    init = (qk(0, s0), jnp.ones((1, blk), F32), jnp.full((1, blk), NEG, F32), jnp.zeros((1, blk), F32))
    smax0, a1, m, l = lax.fori_loop(0, pairs, body, init)
    pv(p1, jnp.maximum(2 * pairs - 1, 0), a1)

    def tail_odd(smax0, m, l):
        qk(qi, s1)
        m, l, a = sm(s0[...], smax0, m, l, p0)
        pv(p0, qi - 1, a)
        m, l, a = sm_diag(s1, m, l, p1)
        pv(p1, qi, a)
        return l

    def tail_even(smax0, m, l):
        m, l, a = sm_diag(s0, m, l, p0)
        pv(p0, qi, a)
        return l

    l = lax.cond(qi % 2 == 1, tail_odd, tail_even, smax0, m, l)
    o_ref[...] = (acc_ref[...] / l).astype(o_ref.dtype)


def _attn_scratch(blk, dv):
    return [pltpu.VMEM((blk, blk), F32), pltpu.VMEM((blk, blk), F32), pltpu.VMEM((blk, blk), BF16),
            pltpu.VMEM((blk, blk), BF16), pltpu.VMEM((dv, blk), F32)]


N_MAPS = 8
RING = 3


def _attn_decode_kernel(pt_ref, q_ref, k_ref, vT_ref, dq_ref, dqd_ref, dckv_ref, dkr_ref, ddk_ref, ddv_ref,
                        ckv_hbm, krT_hbm, dkT_hbm, dv_hbm, o_ref, olat_ref, od_ref,
                        s0, s1, p0, p1, acc_ref, bckv, bkr, bdk, bdv, sem, m1, l1, a1, m2, l2, a2,
                        *, blk, dv, half_axis, nb, pages, nchunk, req0, total):
    if half_axis is None:
        mp, qi = pl.program_id(0), pl.program_id(1)
    else:
        mp, qi = 2 * pl.program_id(0) + pl.program_id(1), pl.program_id(2)
    tri = nb * (nb + 1) // 2
    before = mp * tri + (qi * (qi + 1)) // 2
    lo = (before * total) // (N_MAPS * tri)
    hi = ((before + qi + 1) * total) // (N_MAPS * tri)
    first = jnp.logical_and(mp == 0, qi == 0)
    _decode_chunks(lo, hi, first, pt_ref, dq_ref, dqd_ref, dckv_ref, dkr_ref, ddk_ref, ddv_ref,
                   ckv_hbm, krT_hbm, dkT_hbm, dv_hbm, olat_ref, od_ref,
                   bckv, bkr, bdk, bdv, sem, m1, l1, a1, m2, l2, a2,
                   pages=pages, nchunk=nchunk, req0=req0, total=total)
    _attn_body(q_ref, k_ref, vT_ref, o_ref, s0, s1, p0, p1, acc_ref, blk=blk, dv=dv, half_axis=half_axis)


def _attn_decode(diff, q, k, vT, blk, page_table, dec_in, caches, req0, pages):
    s = q.shape[0]
    nb = s // blk
    nreq = dec_in[0].shape[0]
    nchunk = page_table.shape[1] // pages
    kc = pages * PAGE
    if diff:
        dv, half_axis, grid = 2 * DIFF_HD, 1, (DIFF_HEADS, 2, nb)
        qmap = lambda h, j, i, pt: (i, h)
        kmap = lambda h, j, i, pt: (0, h)
        vmap = lambda h, j, i, pt: (0, h, 0)
        omap = lambda h, j, i, pt: (2 * h + j, i)
        out_dtype = F32
    else:
        dv, half_axis, grid = MLA_V, None, (MLA_HEADS, nb)
        qmap = lambda h, i, pt: (i, h)
        kmap = lambda h, i, pt: (0, h)
        vmap = lambda h, i, pt: (0, h, 0)
        omap = lambda h, i, pt: (h, i)
        out_dtype = BF16
    whole = lambda a: pl.BlockSpec(a.shape, lambda *_: (0,) * a.ndim)
    anyspec = pl.BlockSpec(memory_space=pl.ANY)
    out_shape = [jax.ShapeDtypeStruct((N_MAPS * dv, s), out_dtype),
                 jax.ShapeDtypeStruct((nreq, 8, 256), F32), jax.ShapeDtypeStruct((nreq, 8, 512), F32)]
    grid_spec = pltpu.PrefetchScalarGridSpec(
        num_scalar_prefetch=1,
        grid=grid,
        in_specs=[pl.BlockSpec((blk, LANES), qmap),
                  pl.BlockSpec((s, LANES), kmap, pipeline_mode=pl.Buffered(1)),
                  pl.BlockSpec((nb, dv, blk), vmap, pipeline_mode=pl.Buffered(1))]
        + [whole(a) for a in dec_in] + [anyspec] * 4,
        out_specs=[pl.BlockSpec((dv, blk), omap), whole(out_shape[1]), whole(out_shape[2])],
        scratch_shapes=_attn_scratch(blk, dv) + [
            pltpu.VMEM((RING, kc, 256), F32), pltpu.VMEM((RING, MLA_ROPE, kc), F32),
            pltpu.VMEM((RING, 512, kc), F32), pltpu.VMEM((RING, 4 * kc, LANES), F32),
            pltpu.SemaphoreType.DMA((RING, 4)),
            pltpu.VMEM((8, 1), F32), pltpu.VMEM((8, 1), F32), pltpu.VMEM((8, 256), F32),
            pltpu.VMEM((8, 1), F32), pltpu.VMEM((8, 1), F32), pltpu.VMEM((DIFF_HEADS, 8, LANES), F32)])
    return pl.pallas_call(
        functools.partial(_attn_decode_kernel, blk=blk, dv=dv, half_axis=half_axis, nb=nb, pages=pages,
                          nchunk=nchunk, req0=req0, total=nreq * nchunk),
        grid_spec=grid_spec,
        out_shape=out_shape,
        compiler_params=_params(("arbitrary",) * len(grid)),
        name="attn_diff_decode" if diff else "attn_mla_decode",
    )(page_table, q, k, vT, *dec_in, *caches)


def _decode_chunks(lo, hi, first, pt_ref, q_ref, qd_ref, ckvn_ref, krn_ref, dkn_ref, dvn_ref,
                   ckv_hbm, krT_hbm, dkT_hbm, dv_hbm, olat_ref, od_ref,
                   bckv, bkr, bdk, bdv, sem, m1, l1, a1, m2, l2, a2, *, pages, nchunk, req0, total):
    kc = pages * PAGE

    def copies(c, sl):
        bb = req0 + c // nchunk
        cc = c % nchunk
        out = []
        for g in range(pages):
            pid = pt_ref[bb, cc * pages + g]
            out.append(pltpu.make_async_copy(ckv_hbm.at[pid], bckv.at[sl, pl.ds(g * PAGE, PAGE), :], sem.at[sl, 0]))
            out.append(pltpu.make_async_copy(krT_hbm.at[pid], bkr.at[sl, :, pl.ds(g * PAGE, PAGE)], sem.at[sl, 1]))
            out.append(pltpu.make_async_copy(dkT_hbm.at[pid], bdk.at[sl, :, pl.ds(g * PAGE, PAGE)], sem.at[sl, 2]))
            out.append(pltpu.make_async_copy(dv_hbm.at[pid], bdv.at[sl, pl.ds(g * 4 * PAGE, 4 * PAGE), :],
                                             sem.at[sl, 3]))
        return out

    @pl.when(first)
    def _():
        for r in range(min(RING - 1, total)):
            for cp in copies(r, r):
                cp.start()

    def chunk(c, carry):
        slot = c % RING
        b = c // nchunk
        cc = c % nchunk
        nxt = c + RING - 1

        @pl.when(nxt < total)
        def _():
            for cp in copies(nxt, nxt % RING):
                cp.start()

        @pl.when(cc == 0)
        def _():
            m1[...] = jnp.full(m1.shape, NEG, F32)
            l1[...] = jnp.zeros(l1.shape, F32)
            a1[...] = jnp.zeros(a1.shape, F32)
            m2[...] = jnp.full(m2.shape, NEG, F32)
            l2[...] = jnp.zeros(l2.shape, F32)
            a2[...] = jnp.zeros(a2.shape, F32)

        for cp in copies(c, slot):
            cp.wait()

        q = q_ref[b]
        qa = q[:, :MLA_KV_LORA]
        qr = q[:, MLA_KV_LORA:MLA_KV_LORA + MLA_ROPE]
        row = lax.broadcasted_iota(jnp.int32, (8, 512), 0)
        blkid = lax.broadcasted_iota(jnp.int32, (8, 512), 1) // DIFF_HD
        qbd = jnp.where(row == blkid, jnp.broadcast_to(qd_ref[b], (8, 512)), 0.0)

        ckv = bckv[slot]
        s = lax.dot_general(qa, ckv, (((1,), (1,)), ((), ())), preferred_element_type=F32)
        s = s + jnp.dot(qr, bkr[slot], preferred_element_type=F32)
        m_new = jnp.maximum(m1[...], jnp.max(s, axis=1, keepdims=True))
        alpha = jnp.exp(m1[...] - m_new)
        p = jnp.exp(s - m_new)
        l1[...] = alpha * l1[...] + jnp.sum(p, axis=1, keepdims=True)
        a1[...] = alpha * a1[...] + jnp.dot(p, ckv, preferred_element_type=F32)
        m1[...] = m_new

        s = jnp.dot(qbd, bdk[slot], preferred_element_type=F32)
        m_new = jnp.maximum(m2[...], jnp.max(s, axis=1, keepdims=True))
        alpha = jnp.exp(m2[...] - m_new)
        p = jnp.exp(s - m_new)
        l2[...] = alpha * l2[...] + jnp.sum(p, axis=1, keepdims=True)
        for hh in range(DIFF_HEADS):
            vh = bdv[slot, pl.ds(hh, kc, stride=DIFF_HEADS), :]
            a2[hh] = alpha * a2[hh] + jnp.dot(p, vh, preferred_element_type=F32)
        m2[...] = m_new

        @pl.when(cc == nchunk - 1)
        def _():
            ckvn = ckvn_ref[b]
            s_new = (jnp.sum(qa * ckvn, axis=1, keepdims=True)
                     + jnp.sum(qr * krn_ref[b], axis=1, keepdims=True))
            m_f = jnp.maximum(m1[...], s_new)
            al = jnp.exp(m1[...] - m_f)
            pn = jnp.exp(s_new - m_f)
            inv = 1.0 / (al * l1[...] + pn)
            olat_ref[b] = (al * a1[...] + pn * ckvn) * inv
            s_new = jnp.sum(qbd * dkn_ref[b], axis=1, keepdims=True)
            m_f = jnp.maximum(m2[...], s_new)
            al = jnp.exp(m2[...] - m_f)
            pn = jnp.exp(s_new - m_f)
            inv = 1.0 / (al * l2[...] + pn)
            dvn = dvn_ref[b]
            od_ref[b] = jnp.concatenate(
                [(al * a2[hh] + pn * dvn[:, hh * 2 * DIFF_HD:(hh + 1) * 2 * DIFF_HD]) * inv
                 for hh in range(DIFF_HEADS)], axis=1)

        return carry

    lax.fori_loop(lo, hi, chunk, 0)


def _post_kernel(prompt, x_ref, sc2_ref, sh2_ref, g2_ref, sc3_ref, sh3_ref, g3_ref, om_ref, od_ref,
                 wg_ref, wbm_ref, wbd_ref, wo_ref, subg_ref, dl_ref, lng2_ref, lnb2_ref, lng3_ref, lnb3_ref,
                 w1_ref, w2_ref, wuv_ref, o_ref):
    x = x_ref[...]
    h = (x * (1.0 + sc2_ref[...]) + sh2_ref[...]).astype(BF16)
    gl = jnp.dot(h, wg_ref[...], preferred_element_type=F32)
    gate_m = jax.nn.sigmoid(gl[:, :D_MODEL])
    gate_d = jax.nn.sigmoid(gl[:, D_MODEL:])

    dl = dl_ref[...]
    lam = (jnp.exp(jnp.sum(dl[0:1] * dl[1:2], axis=1, keepdims=True))
           - jnp.exp(jnp.sum(dl[2:3] * dl[3:4], axis=1, keepdims=True)) + LAMBDA_INIT)
    dv = 2 * DIFF_HD
    if prompt:
        o_mla = om_ref[...].astype(F32).T.astype(BF16)
        b_mla = jnp.dot(o_mla, wbm_ref[...], preferred_element_type=F32)
        od = od_ref[...].T
        maps = [od[:, j * dv:(j + 1) * dv] for j in range(2 * DIFF_HEADS)]
    else:
        b_mla = None
        for hh in range(MLA_HEADS):
            o_h = jnp.dot(om_ref[:, hh * MLA_KV_LORA:(hh + 1) * MLA_KV_LORA].astype(BF16), wuv_ref[hh],
                          preferred_element_type=F32)
            part = jnp.dot(o_h.astype(BF16), wbm_ref[hh * MLA_V:(hh + 1) * MLA_V, :], preferred_element_type=F32)
            b_mla = part if b_mla is None else b_mla + part
        maps = [od_ref[:, j * 512 + (j // 2) * dv:j * 512 + (j // 2 + 1) * dv] for j in range(2 * DIFF_HEADS)]
    b_diff = None
    for hh in range(DIFF_HEADS):
        a = maps[2 * hh] - lam * maps[2 * hh + 1]
        a = a * _rms_scale(a) * subg_ref[...] * (1.0 - LAMBDA_INIT)
        part = jnp.dot(a.astype(BF16), wbd_ref[hh * dv:(hh + 1) * dv, :], preferred_element_type=F32)
        b_diff = part if b_diff is None else b_diff + part
    mix = jnp.dot((gate_m * b_mla + gate_d * b_diff).astype(BF16), wo_ref[...], preferred_element_type=F32)
    x2 = _layernorm(ALPHA * x + (1.0 + g2_ref[...]) * mix, lng2_ref[...], lnb2_ref[...])

    xm = (x2 * (1.0 + sc3_ref[...]) + sh3_ref[...]).astype(BF16)
    f = _swiglu(xm, w1_ref, w2_ref)
    o_ref[...] = _layernorm(ALPHA * x2 + 0.5 * (1.0 + g3_ref[...]) * f, lng3_ref[...], lnb3_ref[...])


def _post_prompt_kernel(*refs):
    _post_kernel(True, *refs[:21], None, refs[21])


def _post_sample_kernel(*refs):
    _post_kernel(False, *refs)


def _post(prompt, x, mod, om, od, wg, wbm, wbd, wo, subg, dl, lns, w1, w2, wuv3, tm):
    n = x.shape[0]
    ln_g = lns[0]
    if prompt:
        om_spec = pl.BlockSpec((om.shape[0], tm), lambda i: (0, i))
        od_spec = pl.BlockSpec((od.shape[0], tm), lambda i: (0, i))
    else:
        om_spec = pl.BlockSpec((tm, om.shape[1]), lambda i: (i, 0))
        od_spec = pl.BlockSpec((tm, od.shape[1]), lambda i: (i, 0))
    in_specs = [pl.BlockSpec((tm, D_MODEL), lambda i: (i, 0)),
                _mod_spec(mod, 4, tm), _mod_spec(mod, 3, tm), _mod_spec(mod, 5, tm),
                _mod_spec(mod, 7, tm), _mod_spec(mod, 6, tm), _mod_spec(mod, 8, tm),
                om_spec, od_spec,
                _const_spec(wg.shape), _const_spec(wbm.shape), _const_spec(wbd.shape), _const_spec(wo.shape),
                _const_spec(subg.shape), _const_spec(dl.shape),
                _const_spec(ln_g.shape), _const_spec(ln_g.shape), _const_spec(ln_g.shape), _const_spec(ln_g.shape),
                _const_spec(w1.shape), _const_spec(w2.shape)]
    args = [x, mod, mod, mod, mod, mod, mod, om, od, wg, wbm, wbd, wo, subg, dl, *lns, w1, w2]
    if not prompt:
        in_specs.append(_const_spec(wuv3.shape))
        args.append(wuv3)
    return pl.pallas_call(
        _post_prompt_kernel if prompt else _post_sample_kernel,
        grid=(n // tm,),
        in_specs=in_specs,
        out_specs=pl.BlockSpec((tm, D_MODEL), lambda i: (i, 0)),
        out_shape=jax.ShapeDtypeStruct((n, D_MODEL), F32),
        compiler_params=_params(("arbitrary",)),
        name="post_prompt" if prompt else "post_sample",
    )(*args)


def _pack_weights(w_in, w_uq, w_uk, w_uv):
    o = [0]
    for w in (MLA_Q_LORA, MLA_KV_LORA, MLA_ROPE, 512, 512, 512, D_MODEL, D_MODEL):
        o.append(o[-1] + w)
    w_qlat, w_ckv, w_kr, w_qd, w_kd, w_vd, w_gm, w_gd = (w_in[:, o[i]:o[i + 1]] for i in range(8))
    half = MLA_ROPE // 2
    swap32 = jnp.concatenate([jnp.arange(half, MLA_ROPE), jnp.arange(0, half)])
    w_kr_sw = w_kr[:, swap32]
    col = jnp.arange(512)
    within = col % DIFF_HD
    partner = jnp.where(within < DIFF_ROT // 2, col + DIFF_ROT // 2, col - DIFF_ROT // 2)
    partner = jnp.where(within < DIFF_ROT, partner, col)
    live = (within < DIFF_ROT).astype(w_in.dtype)
    w_qd_sw = w_qd[:, partner] * live
    w_kd_sw = w_kd[:, partner] * live
    wc = jnp.concatenate([w_qlat, w_ckv, w_kr, w_kr_sw, w_kr, w_kr_sw, w_qd, w_qd_sw, w_kd, w_kd_sw, w_vd],
                         axis=1).astype(BF16)
    wg = jnp.concatenate([w_gm, w_gd], axis=1).astype(BF16)
    nope, rope = w_uq[:, :, :MLA_NOPE], w_uq[:, :, MLA_NOPE:]
    wuq = jnp.concatenate([nope, rope, rope[:, :, swap32]], axis=2).reshape(MLA_Q_LORA, MLA_HEADS * LANES).astype(BF16)
    wuk_pad = jnp.pad(w_uk, ((0, 0), (0, 0), (0, LANES - MLA_NOPE)))
    wk_prompt = wuk_pad.reshape(MLA_KV_LORA, MLA_HEADS * LANES).astype(BF16)
    wk_sample = jnp.transpose(wuk_pad, (1, 2, 0)).astype(BF16)
    wuv = w_uv.reshape(MLA_KV_LORA, MLA_HEADS * MLA_V).astype(BF16)
    wuv3 = jnp.transpose(w_uv, (1, 0, 2)).astype(BF16)
    return wc, wg, wuq, wk_prompt, wk_sample, wuv, wuv3


def _inv_rows():
    lane = jnp.arange(LANES)
    inv_m = ROPE_THETA ** (-jnp.arange(0, MLA_ROPE, 2, dtype=F32) / MLA_ROPE)
    inv_d = ROPE_THETA ** (-jnp.arange(0, DIFF_ROT, 2, dtype=F32) / DIFF_ROT)
    invm = inv_m[lane % (MLA_ROPE // 2)][None, :]
    within = lane % DIFF_HD
    invd = jnp.where(within < DIFF_ROT, inv_d[within % (DIFF_ROT // 2)], 0.0)[None, :]
    return invm, invd


def kernel(x_prompt, x_sample, cache_ckv, cache_krope, cache_diff_k, cache_diff_v, page_table, c_prompt, c_sample,
           w_ada, b_ada, ln_g, ln_b, ffn_w1, ffn_w2, w_in, q_norm_g, kv_norm_g, w_uq, w_uk, w_uv, diff_lambda,
           subln_g, w_br_mla, w_br_diff, w_out):
    assert w_ada.shape[0] == DEPTH and x_prompt.shape[0] == 1 and x_sample.shape[1] == 1
    s = x_prompt.shape[1]
    dec = x_sample.shape[0]
    n_pool, page = cache_ckv.shape[1], cache_ckv.shape[2]
    assert page == PAGE
    past_len = page_table.shape[1] * page
    tm = min(512, s)
    tp = min(256, s)
    pages = min(16, page_table.shape[1])

    wc, wg, wuq, wk_p, wk_s, wuv, wuv3 = _pack_weights(w_in[0], w_uq[0], w_uk[0], w_uv[0])
    w1 = ffn_w1[0].astype(BF16)
    w2 = ffn_w2[0].astype(BF16)
    wbm = w_br_mla[0].astype(BF16)
    wbd = w_br_diff[0].astype(BF16)
    wo = w_out[0].astype(BF16)
    lng = [ln_g[0, k:k + 1] for k in range(3)]
    lnb = [ln_b[0, k:k + 1] for k in range(3)]
    lns = (lng[1], lnb[1], lng[2], lnb[2])
    qg, kvg = q_norm_g, kv_norm_g
    dl = diff_lambda[0]
    invm, invd = _inv_rows()

    rows = 1 + dec
    rpad = -rows % 8
    c_all = jnp.concatenate([c_prompt, c_sample, jnp.zeros((rpad, D_MODEL), F32)], axis=0)
    mod = _adaln(c_all, w_ada[0], b_ada)
    mod_p, mod_s = mod[0:1], mod[1:rows]

    xp = x_prompt[0]
    pos_p = jnp.arange(s, dtype=jnp.int32).astype(F32)[:, None]
    x1p = _ffn_ln(xp, mod_p, lng[0], lnb[0], w1[0], w2[0], tm)
    (ckv_p, krT_p, dkT_p, dv_p, qm, km, qd, kd, vTm, vTd) = _mixer_in(
        True, x1p, mod_p, pos_p, invm, invd, wc, qg, kvg, wuq, wk_p, wuv, tm)
    xs = x_sample[:, 0, :]
    pos_s = jnp.full((dec, 1), past_len, jnp.int32).astype(F32)
    x1s = _ffn_ln(xs, mod_s, lng[0], lnb[0], w1[0], w2[0], dec)
    (ckv_s, kr_s, dk_s, dv_s, qcat, qd_s) = _mixer_in(
        False, x1s, mod_s, pos_s, invm, invd, wc, qg, kvg, wuq, wk_s, None, dec)
    caches = (cache_ckv[0],
              jnp.transpose(cache_krope[0], (0, 2, 1)),
              jnp.transpose(cache_diff_k[0], (0, 2, 3, 4, 1)).reshape(n_pool, 512, page),
              cache_diff_v[0].reshape(n_pool, page * DIFF_HEADS, 2 * DIFF_HD))
    dec_all = (qcat.reshape(dec, 8, 384), qd_s[:, None, :], ckv_s[:, None, :], kr_s[:, None, :],
               dk_s[:, None, :], dv_s[:, None, :])

    split = dec // 2
    oTm, o_lat_a, o_dd_a = _attn_decode(False, qm, km, vTm, tm, page_table, tuple(a[:split] for a in dec_all),
                                        caches, 0, pages)
    oTd, o_lat_b, o_dd_b = _attn_decode(True, qd, kd, vTd, tm, page_table, tuple(a[split:] for a in dec_all),
                                        caches, split, pages)
    o_lat = jnp.concatenate([o_lat_a, o_lat_b], axis=0)
    o_dd = jnp.concatenate([o_dd_a, o_dd_b], axis=0)
    y_p = _post(True, x1p, mod_p, oTm, oTd, wg, wbm, wbd, wo, subln_g, dl, lns, w1[1], w2[1], None, tp)
    y_s = _post(False, x1s, mod_s, o_lat.reshape(dec, 8 * 256), o_dd.reshape(dec, 8 * 512), wg, wbm, wbd, wo,
                subln_g, dl, lns, w1[1], w2[1], wuv3, dec)

    new_ckv_p = ckv_p[None, None]
    new_kr_p = krT_p.T[None, None]
    new_dk_p = dkT_p.reshape(DIFF_HEADS, 2, DIFF_HD, s).transpose(3, 0, 1, 2)[None, None]
    new_dv_p = dv_p.reshape(1, 1, s, DIFF_HEADS, 2 * DIFF_HD)
    new_ckv_s = ckv_s[None, :, None, :]
    new_kr_s = kr_s[None, :, None, :]
    new_dk_s = dk_s.reshape(1, dec, 1, DIFF_HEADS, 2, DIFF_HD)
    new_dv_s = dv_s.reshape(1, dec, 1, DIFF_HEADS, 2 * DIFF_HD)
    return (y_p[None], y_s[:, None, :], new_ckv_p, new_kr_p, new_dk_p, new_dv_p,
            new_ckv_s, new_kr_s, new_dk_s, new_dv_s)
```

```python
import functools
import math

import jax
import jax.numpy as jnp
from jax import lax
from jax.experimental import pallas as pl
from jax.experimental.pallas import tpu as pltpu

F32 = jnp.float32
BF16 = jnp.bfloat16

D_MODEL = 1024
D_FF = 2816
N_MOD = 9
MLA_HEADS = 8
MLA_Q_LORA = 384
MLA_KV_LORA = 256
MLA_NOPE = 64
MLA_ROPE = 32
MLA_V = 64
DIFF_HEADS = 4
DIFF_HD = 64
DIFF_ROT = 16
ROPE_THETA = 500000.0
DEPTH = 1
ALPHA = (2 * DEPTH) ** 0.25
LN_EPS = 1e-5
RMS_EPS = 1e-6
MLA_SCALE = (MLA_NOPE + MLA_ROPE) ** -0.5
DIFF_SCALE = DIFF_HD ** -0.5
LAMBDA_INIT = 0.8 - 0.6 * math.exp(-0.3 * 0)
LOG2E = math.log2(math.e)
PAGE = 128
LANES = 128
NEG = -1e30

VMEM_LIMIT = 56 * 1024 * 1024
C_QLAT, C_CKV, C_KR4, C_QD, C_QDSW, C_KD, C_KDSW, C_VD, C_END = (
    0, 384, 640, 768, 1280, 1792, 2304, 2816, 3328)


def _params(sem=None):
    return pltpu.CompilerParams(dimension_semantics=sem, vmem_limit_bytes=VMEM_LIMIT)


def _const_spec(shape):
    nd = len(shape)
    return pl.BlockSpec(shape, lambda *_: (0,) * nd, pipeline_mode=pl.Buffered(1))


def _layernorm(y, g, b):
    mu = jnp.mean(y, axis=-1, keepdims=True)
    d = y - mu
    var = jnp.mean(d * d, axis=-1, keepdims=True)
    return d * lax.rsqrt(var + LN_EPS) * g + b


def _rms_scale(x):
    return lax.rsqrt(jnp.mean(x * x, axis=-1, keepdims=True) + RMS_EPS)


def _swiglu(xm, w1_ref, w2_ref):
    f = w2_ref.shape[0]
    fc = f // 2
    acc = None
    for j in range(2):
        gate = jnp.dot(xm, w1_ref[:, j * fc:(j + 1) * fc], preferred_element_type=F32)
        up = jnp.dot(xm, w1_ref[:, f + j * fc:f + (j + 1) * fc], preferred_element_type=F32)
        act = (gate * jax.nn.sigmoid(gate) * up).astype(BF16)
        part = jnp.dot(act, w2_ref[j * fc:(j + 1) * fc, :], preferred_element_type=F32)
        acc = part if acc is None else acc + part
    return acc


def _adaln_kernel(c_ref, w_ref, b_ref, o_ref):
    c = c_ref[...]
    a = (c * jax.nn.sigmoid(c)).astype(BF16)
    o_ref[...] = jnp.dot(a, w_ref[...].astype(BF16), preferred_element_type=F32) + b_ref[...]


def _adaln(c_all, w_ada, b_ada):
    r = c_all.shape[0]
    n = w_ada.shape[1]
    tn = D_MODEL
    return pl.pallas_call(
        _adaln_kernel,
        grid=(n // tn,),
        in_specs=[pl.BlockSpec((r, D_MODEL), lambda j: (0, 0)),
                  pl.BlockSpec((D_MODEL, tn), lambda j: (0, j)),
                  pl.BlockSpec((1, tn), lambda j: (0, j))],
        out_specs=pl.BlockSpec((r, tn), lambda j: (0, j)),
        out_shape=jax.ShapeDtypeStruct((r, n), F32),
        compiler_params=_params(("arbitrary",)),
        name="adaln",
    )(c_all, w_ada, b_ada)


def _mod_spec(mod, k, tm):
    if mod.shape[0] == 1:
        return pl.BlockSpec((1, D_MODEL), lambda i: (0, k))
    return pl.BlockSpec((tm, D_MODEL), lambda i: (i, k))


def _ffn_ln_kernel(x_ref, sc_ref, sh_ref, g_ref, lng_ref, lnb_ref, w1_ref, w2_ref, o_ref):
    x = x_ref[...]
    xm = (x * (1.0 + sc_ref[...]) + sh_ref[...]).astype(BF16)
    f = _swiglu(xm, w1_ref, w2_ref)
    y = ALPHA * x + 0.5 * (1.0 + g_ref[...]) * f
    o_ref[...] = _layernorm(y, lng_ref[...], lnb_ref[...])


def _ffn_ln(x, mod, ln_g, ln_b, w1, w2, tm):
    n = x.shape[0]
    return pl.pallas_call(
        _ffn_ln_kernel,
        grid=(n // tm,),
        in_specs=[pl.BlockSpec((tm, D_MODEL), lambda i: (i, 0)),
                  _mod_spec(mod, 1, tm), _mod_spec(mod, 0, tm), _mod_spec(mod, 2, tm),
                  _const_spec(ln_g.shape), _const_spec(ln_b.shape),
                  _const_spec(w1.shape), _const_spec(w2.shape)],
        out_specs=pl.BlockSpec((tm, D_MODEL), lambda i: (i, 0)),
        out_shape=jax.ShapeDtypeStruct((n, D_MODEL), F32),
        compiler_params=_params(("arbitrary",)),
        name="ffn_ln",
    )(x, mod, mod, mod, ln_g, ln_b, w1, w2)


def _rope_tables(pos, invm, invd, tm):
    lane = lax.broadcasted_iota(jnp.int32, (tm, LANES), 1)
    ang = pos * invm
    sgn = jnp.where((lane % 32) < 16, -1.0, 1.0)
    tk = jnp.where((lane // 32) % 2 == 0, jnp.cos(ang), jnp.sin(ang) * sgn)
    angd = pos * invd
    sgnd = jnp.where((lane % 64) < 8, -1.0, 1.0)
    return lane, tk, jnp.cos(angd), jnp.sin(angd) * sgnd


def _mixer_in_body(prompt, x_ref, sc_ref, sh_ref, pos_ref, invm_ref, invd_ref, wc_ref, qg_ref, kvg_ref,
                   wuq_ref, wk_ref, wuv_ref, outs):
    tm = x_ref.shape[0]
    x = x_ref[...]
    h = (x * (1.0 + sc_ref[...]) + sh_ref[...]).astype(BF16)
    proj = jnp.dot(h, wc_ref[...], preferred_element_type=F32)
    q_lat = proj[:, C_QLAT:C_CKV]
    ckv = proj[:, C_CKV:C_KR4]
    kr4 = proj[:, C_KR4:C_QD]
    q_d = proj[:, C_QD:C_QDSW]
    q_dsw = proj[:, C_QDSW:C_KD]
    k_d = proj[:, C_KD:C_KDSW]
    k_dsw = proj[:, C_KDSW:C_VD]
    v_d = proj[:, C_VD:C_END]

    lane, tk, cd, sd = _rope_tables(pos_ref[...], invm_ref[...], invd_ref[...], tm)
    cd4 = jnp.concatenate([cd] * 4, axis=1)
    sd4 = jnp.concatenate([sd] * 4, axis=1)
    qscale = LOG2E if prompt else 1.0
    q_d_rot = (q_d * cd4 + q_dsw * sd4) * (DIFF_SCALE * qscale)
    k_d_rot = k_d * cd4 + k_dsw * sd4

    kr_t = kr4 * tk
    kr_rot = kr_t + pltpu.roll(kr_t, 96, 1)

    ckv_n = ckv * _rms_scale(ckv) * kvg_ref[...]
    ckv_b = ckv_n.astype(BF16)
    qn = (q_lat * _rms_scale(q_lat) * qg_ref[...]).astype(BF16)
    tq = jnp.where(lane < MLA_NOPE, MLA_SCALE * qscale, tk * (MLA_SCALE * qscale))
    q_full = jnp.dot(qn, wuq_ref[...], preferred_element_type=F32) * jnp.concatenate([tq] * MLA_HEADS, axis=1)

    if prompt:
        (ckv_o, krT_o, dkT_o, dv_o, qm_o, km_o, qd_o, kd_o, vTm_o, vTd_o) = outs
        ckv_o[...] = ckv_n
        krT_o[...] = kr_rot.T[:MLA_ROPE, :]
        dkT_o[...] = k_d_rot.T
        dv_o[...] = v_d
        qm_o[...] = q_full.astype(BF16)
        kr_hi = jnp.where(lane >= MLA_NOPE, kr_rot, 0.0)
        k_nope = jnp.dot(ckv_b, wk_ref[...], preferred_element_type=F32)
        km_o[...] = (k_nope + jnp.concatenate([kr_hi] * MLA_HEADS, axis=1)).astype(BF16)
        qd_o[...] = q_d_rot.astype(BF16)
        kd_o[...] = k_d_rot.astype(BF16)
        v_mla = jnp.dot(ckv_b, wuv_ref[...], preferred_element_type=F32)
        vTm_o[0] = v_mla.T.astype(BF16)
        vTd_o[0] = v_d.T.astype(BF16)
    else:
        (ckv_o, kr_o, dk_o, dv_o, qcat_o, qd_o) = outs
        ckv_o[...] = ckv_n
        kr_o[...] = kr_rot[:, :MLA_ROPE]
        dk_o[...] = k_d_rot
        dv_o[...] = v_d
        qd_o[...] = q_d_rot
        for hh in range(MLA_HEADS):
            blk = q_full[:, hh * LANES:(hh + 1) * LANES]
            q_abs = jnp.dot(blk.astype(BF16), wk_ref[hh], preferred_element_type=F32)
            r = blk + pltpu.roll(blk, 96, 1)
            q_rot = jnp.where(lane < MLA_ROPE, pltpu.roll(r, 64, 1), 0.0)
            qcat_o[:, hh * 384:hh * 384 + 256] = q_abs
            qcat_o[:, hh * 384 + 256:(hh + 1) * 384] = q_rot


def _mixer_in_prompt_kernel(x_ref, sc_ref, sh_ref, pos_ref, invm_ref, invd_ref, wc_ref, qg_ref, kvg_ref,
                            wuq_ref, wk_ref, wuv_ref, *outs):
    _mixer_in_body(True, x_ref, sc_ref, sh_ref, pos_ref, invm_ref, invd_ref, wc_ref, qg_ref, kvg_ref,
                   wuq_ref, wk_ref, wuv_ref, outs)


def _mixer_in_sample_kernel(x_ref, sc_ref, sh_ref, pos_ref, invm_ref, invd_ref, wc_ref, qg_ref, kvg_ref,
                            wuq_ref, wk_ref, *outs):
    _mixer_in_body(False, x_ref, sc_ref, sh_ref, pos_ref, invm_ref, invd_ref, wc_ref, qg_ref, kvg_ref,
                   wuq_ref, wk_ref, None, outs)


def _mixer_in(prompt, x, mod, pos, invm, invd, wc, qg, kvg, wuq, wk, wuv, tm):
    n = x.shape[0]
    nt = n // tm
    row = lambda w: pl.BlockSpec((tm, w), lambda i: (i, 0))
    colT = lambda r: pl.BlockSpec((r, tm), lambda i: (0, i))
    in_specs = [row(D_MODEL), _mod_spec(mod, 4, tm), _mod_spec(mod, 3, tm),
                pl.BlockSpec((tm, 1), lambda i: (i, 0)),
                _const_spec(invm.shape), _const_spec(invd.shape), _const_spec(wc.shape),
                _const_spec(qg.shape), _const_spec(kvg.shape), _const_spec(wuq.shape), _const_spec(wk.shape)]
    args = [x, mod, mod, pos, invm, invd, wc, qg, kvg, wuq, wk]
    sds = jax.ShapeDtypeStruct
    if prompt:
        in_specs.append(_const_spec(wuv.shape))
        args.append(wuv)
        out_specs = [row(256), colT(MLA_ROPE), colT(512), row(512), row(1024), row(1024), row(512), row(512),
                     pl.BlockSpec((1, 512, tm), lambda i: (i, 0, 0)),
                     pl.BlockSpec((1, 512, tm), lambda i: (i, 0, 0))]
        out_shape = [sds((n, 256), F32), sds((MLA_ROPE, n), F32), sds((512, n), F32), sds((n, 512), F32),
                     sds((n, 1024), BF16), sds((n, 1024), BF16), sds((n, 512), BF16), sds((n, 512), BF16),
                     sds((nt, 512, tm), BF16), sds((nt, 512, tm), BF16)]
        body = _mixer_in_prompt_kernel
    else:
        out_specs = [row(256), row(MLA_ROPE), row(512), row(512), row(MLA_HEADS * 384), row(512)]
        out_shape = [sds((n, 256), F32), sds((n, MLA_ROPE), F32), sds((n, 512), F32), sds((n, 512), F32),
                     sds((n, MLA_HEADS * 384), F32), sds((n, 512), F32)]
        body = _mixer_in_sample_kernel
    return pl.pallas_call(
        body, grid=(nt,), in_specs=in_specs, out_specs=out_specs, out_shape=out_shape,
        compiler_params=_params(("arbitrary",)),
        name="mixer_in_prompt" if prompt else "mixer_in_sample",
    )(*args)


def _attn_body(q_ref, k_ref, vT_ref, o_ref, s0, s1, p0, p1, acc_ref, *, blk, dv, half_axis):
    qi = pl.program_id(half_axis + 1 if half_axis is not None else 1)
    q = q_ref[...]
    if half_axis is not None:
        lane = lax.broadcasted_iota(jnp.int32, q.shape, 1)
        q = jnp.where((lane // DIFF_HD) == pl.program_id(half_axis), q, jnp.zeros_like(q))

    def qk(kb, s_ref):
        start = pl.multiple_of(kb * blk, blk)
        s = lax.dot_general(k_ref[pl.ds(start, blk), :], q, (((1,), (1,)), ((), ())), preferred_element_type=F32)
        s_ref[...] = s
        return jnp.max(s, axis=0, keepdims=True)

    def sm(s, smax, m, l, p_ref):
        m_new = jnp.maximum(m, smax)
        alpha = jnp.exp2(m - m_new)
        p = jnp.exp2(s - m_new)
        p_ref[...] = p.astype(BF16)
        return m_new, alpha * l + jnp.sum(p, axis=0, keepdims=True), alpha

    def sm_diag(s_ref, m, l, p_ref):
        key = lax.broadcasted_iota(jnp.int32, (blk, blk), 0)
        qry = lax.broadcasted_iota(jnp.int32, (blk, blk), 1)
        s = jnp.where(key <= qry, s_ref[...], NEG)
        return sm(s, jnp.max(s, axis=0, keepdims=True), m, l, p_ref)

    def pv(p_ref, kb, alpha):
        acc_ref[...] = alpha * acc_ref[...] + jnp.dot(vT_ref[kb], p_ref[...], preferred_element_type=F32)

    p1[...] = jnp.zeros(p1.shape, BF16)
    acc_ref[...] = jnp.zeros(acc_ref.shape, F32)
    pairs = qi // 2

    def body(j, carry):
        smax0, a1, m, l = carry
        pv(p1, jnp.maximum(2 * j - 1, 0), a1)
        smax1 = qk(2 * j + 1, s1)
        m, l, a0 = sm(s0[...], smax0, m, l, p0)
        pv(p0, 2 * j, a0)
        smax0 = qk(2 * j + 2, s0)
        m, l, a1 = sm(s1[...], smax1, m, l, p1)
        return smax0, a1, m, l

    init = (qk(0, s0), jnp.ones((1, blk), F32), jnp.full((1, blk), NEG, F32), jnp.zeros((1, blk), F32))
    smax0, a1, m, l = lax.fori_loop(0, pairs, body, init)
    pv(p1, jnp.maximum(2 * pairs - 1, 0), a1)

    def tail_odd(smax0, m, l):
        qk(qi, s1)
        m, l, a = sm(s0[...], smax0, m, l, p0)
        pv(p0, qi - 1, a)
        m, l, a = sm_diag(s1, m, l, p1)
        pv(p1, qi, a)
        return l

    def tail_even(smax0, m, l):
        m, l, a = sm_diag(s0, m, l, p0)
        pv(p0, qi, a)
        return l

    l = lax.cond(qi % 2 == 1, tail_odd, tail_even, smax0, m, l)
    o_ref[...] = (acc_ref[...] / l).astype(o_ref.dtype)


def _attn_scratch(blk, dv):
    return [pltpu.VMEM((blk, blk), F32), pltpu.VMEM((blk, blk), F32), pltpu.VMEM((blk, blk), BF16),
            pltpu.VMEM((blk, blk), BF16), pltpu.VMEM((dv, blk), F32)]


N_MAPS = 8
RING = 3


def _attn_decode_kernel(pt_ref, q_ref, k_ref, vT_ref, dq_ref, dqd_ref, dckv_ref, dkr_ref, ddk_ref, ddv_ref,
                        ckv_hbm, krT_hbm, dkT_hbm, dv_hbm, o_ref, olat_ref, od_ref,
                        s0, s1, p0, p1, acc_ref, bckv, bkr, bdk, bdv, sem, m1, l1, a1, m2, l2, a2,
                        *, blk, dv, half_axis, nb, pages, nchunk, req0, total):
    if half_axis is None:
        mp, qi = pl.program_id(0), pl.program_id(1)
    else:
        mp, qi = 2 * pl.program_id(0) + pl.program_id(1), pl.program_id(2)
    tri = nb * (nb + 1) // 2
    before = mp * tri + (qi * (qi + 1)) // 2
    lo = (before * total) // (N_MAPS * tri)
    hi = ((before + qi + 1) * total) // (N_MAPS * tri)
    first = jnp.logical_and(mp == 0, qi == 0)
    _decode_chunks(lo, hi, first, pt_ref, dq_ref, dqd_ref, dckv_ref, dkr_ref, ddk_ref, ddv_ref,
                   ckv_hbm, krT_hbm, dkT_hbm, dv_hbm, olat_ref, od_ref,
                   bckv, bkr, bdk, bdv, sem, m1, l1, a1, m2, l2, a2,
                   pages=pages, nchunk=nchunk, req0=req0, total=total)
    _attn_body(q_ref, k_ref, vT_ref, o_ref, s0, s1, p0, p1, acc_ref, blk=blk, dv=dv, half_axis=half_axis)


def _attn_decode(diff, q, k, vT, blk, page_table, dec_in, caches, req0, pages):
    s = q.shape[0]
    nb = s // blk
    nreq = dec_in[0].shape[0]
    nchunk = page_table.shape[1] // pages
    kc = pages * PAGE
    if diff:
        dv, half_axis, grid = 2 * DIFF_HD, 1, (DIFF_HEADS, 2, nb)
        qmap = lambda h, j, i, pt: (i, h)
        kmap = lambda h, j, i, pt: (0, h)
        vmap = lambda h, j, i, pt: (0, h, 0)
        omap = lambda h, j, i, pt: (2 * h + j, i)
        out_dtype = F32
    else:
        dv, half_axis, grid = MLA_V, None, (MLA_HEADS, nb)
        qmap = lambda h, i, pt: (i, h)
        kmap = lambda h, i, pt: (0, h)
        vmap = lambda h, i, pt: (0, h, 0)
        omap = lambda h, i, pt: (h, i)
        out_dtype = BF16
    whole = lambda a: pl.BlockSpec(a.shape, lambda *_: (0,) * a.ndim)
    anyspec = pl.BlockSpec(memory_space=pl.ANY)
    out_shape = [jax.ShapeDtypeStruct((N_MAPS * dv, s), out_dtype),
                 jax.ShapeDtypeStruct((nreq, 8, 256), F32), jax.ShapeDtypeStruct((nreq, 8, 512), F32)]
    grid_spec = pltpu.PrefetchScalarGridSpec(
        num_scalar_prefetch=1,
        grid=grid,
        in_specs=[pl.BlockSpec((blk, LANES), qmap),
                  pl.BlockSpec((s, LANES), kmap, pipeline_mode=pl.Buffered(1)),
                  pl.BlockSpec((nb, dv, blk), vmap, pipeline_mode=pl.Buffered(1))]
        + [whole(a) for a in dec_in] + [anyspec] * 4,
        out_specs=[pl.BlockSpec((dv, blk), omap), whole(out_shape[1]), whole(out_shape[2])],
        scratch_shapes=_attn_scratch(blk, dv) + [
            pltpu.VMEM((RING, kc, 256), F32), pltpu.VMEM((RING, MLA_ROPE, kc), F32),
            pltpu.VMEM((RING, 512, kc), F32), pltpu.VMEM((RING, 4 * kc, LANES), F32),
            pltpu.SemaphoreType.DMA((RING, 4)),
            pltpu.VMEM((8, 1), F32), pltpu.VMEM((8, 1), F32), pltpu.VMEM((8, 256), F32),
            pltpu.VMEM((8, 1), F32), pltpu.VMEM((8, 1), F32), pltpu.VMEM((DIFF_HEADS, 8, LANES), F32)])
    return pl.pallas_call(
        functools.partial(_attn_decode_kernel, blk=blk, dv=dv, half_axis=half_axis, nb=nb, pages=pages,
                          nchunk=nchunk, req0=req0, total=nreq * nchunk),
        grid_spec=grid_spec,
        out_shape=out_shape,
        compiler_params=_params(("arbitrary",) * len(grid)),
        name="attn_diff_decode" if diff else "attn_mla_decode",
    )(page_table, q, k, vT, *dec_in, *caches)


def _decode_chunks(lo, hi, first, pt_ref, q_ref, qd_ref, ckvn_ref, krn_ref, dkn_ref, dvn_ref,
                   ckv_hbm, krT_hbm, dkT_hbm, dv_hbm, olat_ref, od_ref,
                   bckv, bkr, bdk, bdv, sem, m1, l1, a1, m2, l2, a2, *, pages, nchunk, req0, total):
    kc = pages * PAGE

    def copies(c, sl):
        bb = req0 + c // nchunk
        cc = c % nchunk
        out = []
        for g in range(pages):
            pid = pt_ref[bb, cc * pages + g]
            out.append(pltpu.make_async_copy(ckv_hbm.at[pid], bckv.at[sl, pl.ds(g * PAGE, PAGE), :], sem.at[sl, 0]))
            out.append(pltpu.make_async_copy(krT_hbm.at[pid], bkr.at[sl, :, pl.ds(g * PAGE, PAGE)], sem.at[sl, 1]))
            out.append(pltpu.make_async_copy(dkT_hbm.at[pid], bdk.at[sl, :, pl.ds(g * PAGE, PAGE)], sem.at[sl, 2]))
            out.append(pltpu.make_async_copy(dv_hbm.at[pid], bdv.at[sl, pl.ds(g * 4 * PAGE, 4 * PAGE), :],
                                             sem.at[sl, 3]))
        return out

    @pl.when(first)
    def _():
        for r in range(min(RING - 1, total)):
            for cp in copies(r, r):
                cp.start()

    def chunk(c, carry):
        slot = c % RING
        b = c // nchunk
        cc = c % nchunk
        nxt = c + RING - 1

        @pl.when(nxt < total)
        def _():
            for cp in copies(nxt, nxt % RING):
                cp.start()

        @pl.when(cc == 0)
        def _():
            m1[...] = jnp.full(m1.shape, NEG, F32)
            l1[...] = jnp.zeros(l1.shape, F32)
            a1[...] = jnp.zeros(a1.shape, F32)
            m2[...] = jnp.full(m2.shape, NEG, F32)
            l2[...] = jnp.zeros(l2.shape, F32)
            a2[...] = jnp.zeros(a2.shape, F32)

        for cp in copies(c, slot):
            cp.wait()

        q = q_ref[b]
        qa = q[:, :MLA_KV_LORA]
        qr = q[:, MLA_KV_LORA:MLA_KV_LORA + MLA_ROPE]
        row = lax.broadcasted_iota(jnp.int32, (8, 512), 0)
        blkid = lax.broadcasted_iota(jnp.int32, (8, 512), 1) // DIFF_HD
        qbd = jnp.where(row == blkid, jnp.broadcast_to(qd_ref[b], (8, 512)), 0.0)

        ckv = bckv[slot]
        s = lax.dot_general(qa, ckv, (((1,), (1,)), ((), ())), preferred_element_type=F32)
        s = s + jnp.dot(qr, bkr[slot], preferred_element_type=F32)
        sd = jnp.dot(qbd, bdk[slot], preferred_element_type=F32)

        m_new = jnp.maximum(m1[...], jnp.max(s, axis=1, keepdims=True))
        alpha = jnp.exp(m1[...] - m_new)
        p = jnp.exp(s - m_new)
        l1[...] = alpha * l1[...] + jnp.sum(p, axis=1, keepdims=True)
        a1[...] = alpha * a1[...] + jnp.dot(p, ckv, preferred_element_type=F32)
        m1[...] = m_new

        m_new = jnp.maximum(m2[...], jnp.max(sd, axis=1, keepdims=True))
        alpha = jnp.exp(m2[...] - m_new)
        p = jnp.exp(sd - m_new)
        l2[...] = alpha * l2[...] + jnp.sum(p, axis=1, keepdims=True)
        for hh in range(DIFF_HEADS):
            vh = bdv[slot, pl.ds(hh, kc, stride=DIFF_HEADS), :]
            a2[hh] = alpha * a2[hh] + jnp.dot(p, vh, preferred_element_type=F32)
        m2[...] = m_new

        @pl.when(cc == nchunk - 1)
        def _():
            ckvn = ckvn_ref[b]
            s_new = (jnp.sum(qa * ckvn, axis=1, keepdims=True)
                     + jnp.sum(qr * krn_ref[b], axis=1, keepdims=True))
            m_f = jnp.maximum(m1[...], s_new)
            al = jnp.exp(m1[...] - m_f)
            pn = jnp.exp(s_new - m_f)
            inv = 1.0 / (al * l1[...] + pn)
            olat_ref[b] = (al * a1[...] + pn * ckvn) * inv
            s_new = jnp.sum(qbd * dkn_ref[b], axis=1, keepdims=True)
            m_f = jnp.maximum(m2[...], s_new)
            al = jnp.exp(m2[...] - m_f)
            pn = jnp.exp(s_new - m_f)
            inv = 1.0 / (al * l2[...] + pn)
            dvn = dvn_ref[b]
            od_ref[b] = jnp.concatenate(
                [(al * a2[hh] + pn * dvn[:, hh * 2 * DIFF_HD:(hh + 1) * 2 * DIFF_HD]) * inv
                 for hh in range(DIFF_HEADS)], axis=1)

        return carry

    lax.fori_loop(lo, hi, chunk, 0)


def _post_kernel(prompt, x_ref, sc2_ref, sh2_ref, g2_ref, sc3_ref, sh3_ref, g3_ref, om_ref, od_ref,
                 wg_ref, wbm_ref, wbd_ref, wo_ref, subg_ref, dl_ref, lng2_ref, lnb2_ref, lng3_ref, lnb3_ref,
                 w1_ref, w2_ref, wuv_ref, o_ref):
    x = x_ref[...]
    h = (x * (1.0 + sc2_ref[...]) + sh2_ref[...]).astype(BF16)
    gl = jnp.dot(h, wg_ref[...], preferred_element_type=F32)
    gate_m = jax.nn.sigmoid(gl[:, :D_MODEL])
    gate_d = jax.nn.sigmoid(gl[:, D_MODEL:])

    dl = dl_ref[...]
    lam = (jnp.exp(jnp.sum(dl[0:1] * dl[1:2], axis=1, keepdims=True))
           - jnp.exp(jnp.sum(dl[2:3] * dl[3:4], axis=1, keepdims=True)) + LAMBDA_INIT)
    dv = 2 * DIFF_HD
    if prompt:
        o_mla = om_ref[...].astype(F32).T.astype(BF16)
        b_mla = jnp.dot(o_mla, wbm_ref[...], preferred_element_type=F32)
        od = od_ref[...].T
        maps = [od[:, j * dv:(j + 1) * dv] for j in range(2 * DIFF_HEADS)]
    else:
        b_mla = None
        for hh in range(MLA_HEADS):
            o_h = jnp.dot(om_ref[:, hh * MLA_KV_LORA:(hh + 1) * MLA_KV_LORA].astype(BF16), wuv_ref[hh],
                          preferred_element_type=F32)
            part = jnp.dot(o_h.astype(BF16), wbm_ref[hh * MLA_V:(hh + 1) * MLA_V, :], preferred_element_type=F32)
            b_mla = part if b_mla is None else b_mla + part
        maps = [od_ref[:, j * 512 + (j // 2) * dv:j * 512 + (j // 2 + 1) * dv] for j in range(2 * DIFF_HEADS)]
    b_diff = None
    for hh in range(DIFF_HEADS):
        a = maps[2 * hh] - lam * maps[2 * hh + 1]
        a = a * _rms_scale(a) * subg_ref[...] * (1.0 - LAMBDA_INIT)
        part = jnp.dot(a.astype(BF16), wbd_ref[hh * dv:(hh + 1) * dv, :], preferred_element_type=F32)
        b_diff = part if b_diff is None else b_diff + part
    mix = jnp.dot((gate_m * b_mla + gate_d * b_diff).astype(BF16), wo_ref[...], preferred_element_type=F32)
    x2 = _layernorm(ALPHA * x + (1.0 + g2_ref[...]) * mix, lng2_ref[...], lnb2_ref[...])

    xm = (x2 * (1.0 + sc3_ref[...]) + sh3_ref[...]).astype(BF16)
    f = _swiglu(xm, w1_ref, w2_ref)
    o_ref[...] = _layernorm(ALPHA * x2 + 0.5 * (1.0 + g3_ref[...]) * f, lng3_ref[...], lnb3_ref[...])


def _post_prompt_kernel(*refs):
    _post_kernel(True, *refs[:21], None, refs[21])


def _post_sample_kernel(*refs):
    _post_kernel(False, *refs)


def _post(prompt, x, mod, om, od, wg, wbm, wbd, wo, subg, dl, lns, w1, w2, wuv3, tm):
    n = x.shape[0]
    ln_g = lns[0]
    if prompt:
        om_spec = pl.BlockSpec((om.shape[0], tm), lambda i: (0, i))
        od_spec = pl.BlockSpec((od.shape[0], tm), lambda i: (0, i))
    else:
        om_spec = pl.BlockSpec((tm, om.shape[1]), lambda i: (i, 0))
        od_spec = pl.BlockSpec((tm, od.shape[1]), lambda i: (i, 0))
    in_specs = [pl.BlockSpec((tm, D_MODEL), lambda i: (i, 0)),
                _mod_spec(mod, 4, tm), _mod_spec(mod, 3, tm), _mod_spec(mod, 5, tm),
                _mod_spec(mod, 7, tm), _mod_spec(mod, 6, tm), _mod_spec(mod, 8, tm),
                om_spec, od_spec,
                _const_spec(wg.shape), _const_spec(wbm.shape), _const_spec(wbd.shape), _const_spec(wo.shape),
                _const_spec(subg.shape), _const_spec(dl.shape),
                _const_spec(ln_g.shape), _const_spec(ln_g.shape), _const_spec(ln_g.shape), _const_spec(ln_g.shape),
                _const_spec(w1.shape), _const_spec(w2.shape)]
    args = [x, mod, mod, mod, mod, mod, mod, om, od, wg, wbm, wbd, wo, subg, dl, *lns, w1, w2]
    if not prompt:
        in_specs.append(_const_spec(wuv3.shape))
        args.append(wuv3)
    return pl.pallas_call(
        _post_prompt_kernel if prompt else _post_sample_kernel,
        grid=(n // tm,),
        in_specs=in_specs,
        out_specs=pl.BlockSpec((tm, D_MODEL), lambda i: (i, 0)),
        out_shape=jax.ShapeDtypeStruct((n, D_MODEL), F32),
        compiler_params=_params(("arbitrary",)),
        name="post_prompt" if prompt else "post_sample",
    )(*args)


def _pack_weights(w_in, w_uq, w_uk, w_uv):
    o = [0]
    for w in (MLA_Q_LORA, MLA_KV_LORA, MLA_ROPE, 512, 512, 512, D_MODEL, D_MODEL):
        o.append(o[-1] + w)
    w_qlat, w_ckv, w_kr, w_qd, w_kd, w_vd, w_gm, w_gd = (w_in[:, o[i]:o[i + 1]] for i in range(8))
    half = MLA_ROPE // 2
    swap32 = jnp.concatenate([jnp.arange(half, MLA_ROPE), jnp.arange(0, half)])
    w_kr_sw = w_kr[:, swap32]
    col = jnp.arange(512)
    within = col % DIFF_HD
    partner = jnp.where(within < DIFF_ROT // 2, col + DIFF_ROT // 2, col - DIFF_ROT // 2)
    partner = jnp.where(within < DIFF_ROT, partner, col)
    live = (within < DIFF_ROT).astype(w_in.dtype)
    w_qd_sw = w_qd[:, partner] * live
    w_kd_sw = w_kd[:, partner] * live
    wc = jnp.concatenate([w_qlat, w_ckv, w_kr, w_kr_sw, w_kr, w_kr_sw, w_qd, w_qd_sw, w_kd, w_kd_sw, w_vd],
                         axis=1).astype(BF16)
    wg = jnp.concatenate([w_gm, w_gd], axis=1).astype(BF16)
    nope, rope = w_uq[:, :, :MLA_NOPE], w_uq[:, :, MLA_NOPE:]
    wuq = jnp.concatenate([nope, rope, rope[:, :, swap32]], axis=2).reshape(MLA_Q_LORA, MLA_HEADS * LANES).astype(BF16)
    wuk_pad = jnp.pad(w_uk, ((0, 0), (0, 0), (0, LANES - MLA_NOPE)))
    wk_prompt = wuk_pad.reshape(MLA_KV_LORA, MLA_HEADS * LANES).astype(BF16)
    wk_sample = jnp.transpose(wuk_pad, (1, 2, 0)).astype(BF16)
    wuv = w_uv.reshape(MLA_KV_LORA, MLA_HEADS * MLA_V).astype(BF16)
    wuv3 = jnp.transpose(w_uv, (1, 0, 2)).astype(BF16)
    return wc, wg, wuq, wk_prompt, wk_sample, wuv, wuv3


def _inv_rows():
    lane = jnp.arange(LANES)
    inv_m = ROPE_THETA ** (-jnp.arange(0, MLA_ROPE, 2, dtype=F32) / MLA_ROPE)
    inv_d = ROPE_THETA ** (-jnp.arange(0, DIFF_ROT, 2, dtype=F32) / DIFF_ROT)
    invm = inv_m[lane % (MLA_ROPE // 2)][None, :]
    within = lane % DIFF_HD
    invd = jnp.where(within < DIFF_ROT, inv_d[within % (DIFF_ROT // 2)], 0.0)[None, :]
    return invm, invd


def kernel(x_prompt, x_sample, cache_ckv, cache_krope, cache_diff_k, cache_diff_v, page_table, c_prompt, c_sample,
           w_ada, b_ada, ln_g, ln_b, ffn_w1, ffn_w2, w_in, q_norm_g, kv_norm_g, w_uq, w_uk, w_uv, diff_lambda,
           subln_g, w_br_mla, w_br_diff, w_out):
    assert w_ada.shape[0] == DEPTH and x_prompt.shape[0] == 1 and x_sample.shape[1] == 1
    s = x_prompt.shape[1]
    dec = x_sample.shape[0]
    n_pool, page = cache_ckv.shape[1], cache_ckv.shape[2]
    assert page == PAGE
    past_len = page_table.shape[1] * page
    tm = min(512, s)
    tp = min(256, s)
    pages = min(16, page_table.shape[1])

    wc, wg, wuq, wk_p, wk_s, wuv, wuv3 = _pack_weights(w_in[0], w_uq[0], w_uk[0], w_uv[0])
    w1 = ffn_w1[0].astype(BF16)
    w2 = ffn_w2[0].astype(BF16)
    wbm = w_br_mla[0].astype(BF16)
    wbd = w_br_diff[0].astype(BF16)
    wo = w_out[0].astype(BF16)
    lng = [ln_g[0, k:k + 1] for k in range(3)]
    lnb = [ln_b[0, k:k + 1] for k in range(3)]
    lns = (lng[1], lnb[1], lng[2], lnb[2])
    qg, kvg = q_norm_g, kv_norm_g
    dl = diff_lambda[0]
    invm, invd = _inv_rows()

    rows = 1 + dec
    rpad = -rows % 8
    c_all = jnp.concatenate([c_prompt, c_sample, jnp.zeros((rpad, D_MODEL), F32)], axis=0)
    mod = _adaln(c_all, w_ada[0], b_ada)
    mod_p, mod_s = mod[0:1], mod[1:rows]

    xp = x_prompt[0]
    pos_p = jnp.arange(s, dtype=jnp.int32).astype(F32)[:, None]
    x1p = _ffn_ln(xp, mod_p, lng[0], lnb[0], w1[0], w2[0], tm)
    (ckv_p, krT_p, dkT_p, dv_p, qm, km, qd, kd, vTm, vTd) = _mixer_in(
        True, x1p, mod_p, pos_p, invm, invd, wc, qg, kvg, wuq, wk_p, wuv, tm)
    xs = x_sample[:, 0, :]
    pos_s = jnp.full((dec, 1), past_len, jnp.int32).astype(F32)
    x1s = _ffn_ln(xs, mod_s, lng[0], lnb[0], w1[0], w2[0], dec)
    (ckv_s, kr_s, dk_s, dv_s, qcat, qd_s) = _mixer_in(
        False, x1s, mod_s, pos_s, invm, invd, wc, qg, kvg, wuq, wk_s, None, dec)
    caches = (cache_ckv[0],
              jnp.transpose(cache_krope[0], (0, 2, 1)),
              jnp.transpose(cache_diff_k[0], (0, 2, 3, 4, 1)).reshape(n_pool, 512, page),
              cache_diff_v[0].reshape(n_pool, page * DIFF_HEADS, 2 * DIFF_HD))
    dec_all = (qcat.reshape(dec, 8, 384), qd_s[:, None, :], ckv_s[:, None, :], kr_s[:, None, :],
               dk_s[:, None, :], dv_s[:, None, :])

    split = dec // 2
    oTm, o_lat_a, o_dd_a = _attn_decode(False, qm, km, vTm, tm, page_table, tuple(a[:split] for a in dec_all),
                                        caches, 0, pages)
    oTd, o_lat_b, o_dd_b = _attn_decode(True, qd, kd, vTd, tm, page_table, tuple(a[split:] for a in dec_all),
                                        caches, split, pages)
    o_lat = jnp.concatenate([o_lat_a, o_lat_b], axis=0)
    o_dd = jnp.concatenate([o_dd_a, o_dd_b], axis=0)
    y_p = _post(True, x1p, mod_p, oTm, oTd, wg, wbm, wbd, wo, subln_g, dl, lns, w1[1], w2[1], None, tp)
    y_s = _post(False, x1s, mod_s, o_lat.reshape(dec, 8 * 256), o_dd.reshape(dec, 8 * 512), wg, wbm, wbd, wo,
                subln_g, dl, lns, w1[1], w2[1], wuv3, dec)

    new_ckv_p = ckv_p[None, None]
    new_kr_p = krT_p.T[None, None]
    new_dk_p = dkT_p.reshape(DIFF_HEADS, 2, DIFF_HD, s).transpose(3, 0, 1, 2)[None, None]
    new_dv_p = dv_p.reshape(1, 1, s, DIFF_HEADS, 2 * DIFF_HD)
    new_ckv_s = ckv_s[None, :, None, :]
    new_kr_s = kr_s[None, :, None, :]
    new_dk_s = dk_s.reshape(1, dec, 1, DIFF_HEADS, 2, DIFF_HD)
    new_dv_s = dv_s.reshape(1, dec, 1, DIFF_HEADS, 2 * DIFF_HD)
    return (y_p[None], y_s[:, None, :], new_ckv_p, new_kr_p, new_dk_p, new_dv_p,
            new_ckv_s, new_kr_s, new_dk_s, new_dv_s)
```
